```python
import math
import jax, jax.numpy as jnp
from jax import lax
import numpy as np

D_MODEL = 4096
BATCH = 2
SEQ = 8192
DEPTH = 2

CHUNK = 64
EPS = 1e-6
N_EVEN = (DEPTH + 1) // 2
N_ODD = DEPTH // 2

A_WIDTH = D_MODEL // 2
A_KEY = 128
A_HEADS = A_WIDTH // A_KEY
A_VAL = A_WIDTH // A_HEADS
B_WIDTH = D_MODEL // 2
B_GROUP = 16
B_GROUPS = B_WIDTH // B_GROUP
B_STATE = 64
S5_DT_MIN = 1e-3
S5_DT_MAX = 1e-1
AB_IN = 4 * A_WIDTH + B_WIDTH
AB_OUT = A_WIDTH + B_WIDTH
C_HEADS = 16
C_QK = D_MODEL
C_V = 2 * D_MODEL
C_DK = C_QK // C_HEADS
C_DV = C_V // C_HEADS
C_IN = 2 * C_QK + 2 * C_V
ROPE_BASE = 10000.0
PEER_HEADS = 8
PEER_NKEYS = 128
PEER_EXPERTS = PEER_NKEYS * PEER_NKEYS
PEER_TOPK = 16
PEER_KEY_HALF = 128
PEER_QDIM = 2 * PEER_KEY_HALF
PEER_TOKEN_BLOCK = 128

kernel_name = "hybrid_hgrn2_s5_retention_peer"


def rmsnorm(x, g):
    x32 = x.astype(jnp.float32)
    y = x32 * lax.rsqrt(jnp.mean(x32 * x32, axis=-1, keepdims=True) + EPS)
    return (y * g.astype(jnp.float32)).astype(x.dtype)


def head_rms(o):
    return o * lax.rsqrt(jnp.mean(o * o, axis=-1, keepdims=True) + EPS)


def to_chunks(t):
    b, s, h, d = t.shape
    return t.reshape(b, s // CHUNK, CHUNK, h, d).transpose(1, 0, 3, 2, 4)


def from_chunks(t):
    nc, b, h, c, d = t.shape
    return t.transpose(1, 0, 3, 2, 4).reshape(b, nc * c, h, d)


def hgrn2_mixer(q_raw, f_raw, i_raw, g_raw, lb):
    b, s, _ = q_raw.shape
    f = lb + (1.0 - lb) * jax.nn.sigmoid(f_raw.astype(jnp.float32))
    logf = jnp.log(f)
    k = 1.0 - f
    q = jax.nn.silu(q_raw.astype(jnp.float32))
    v = i_raw.astype(jnp.float32)
    qc = to_chunks(q.reshape(b, s, A_HEADS, A_KEY))
    lc = to_chunks(logf.reshape(b, s, A_HEADS, A_KEY))
    kc = to_chunks(k.reshape(b, s, A_HEADS, A_KEY))
    vc = to_chunks(v.reshape(b, s, A_HEADS, A_VAL))
    causal = jnp.tril(jnp.ones((CHUNK, CHUNK), dtype=bool))[:, :, None]

    def step(state, xs):
        qt, lt, kt, vt = xs
        cum = jnp.cumsum(lt, axis=2)
        diff = cum[:, :, :, None, :] - cum[:, :, None, :, :]
        decay = jnp.exp(jnp.where(causal, diff, -jnp.inf))
        attn = jnp.einsum('bhtd,bhtsd,bhsd->bhts', qt, decay, kt)
        o = jnp.einsum('bhts,bhse->bhte', attn, vt) + \
            jnp.einsum('bhtd,bhde->bhte', qt * jnp.exp(cum), state)
        last = cum[:, :, -1:, :]
        state = jnp.exp(last[:, :, 0, :])[..., None] * state + \
            jnp.einsum('bhsd,bhse->bhde', kt * jnp.exp(last - cum), vt)
        return state, o

    s0 = jnp.zeros((b, A_HEADS, A_KEY, A_VAL), jnp.float32)
    _, o = lax.scan(step, s0, (qc, lc, kc, vc))
    o = head_rms(from_chunks(o)).reshape(b, s, A_WIDTH)
    return o * jax.nn.silu(g_raw.astype(jnp.float32))


def _complex_affine_combine(e1, e2):
    a1r, a1i, b1r, b1i = e1
    a2r, a2i, b2r, b2i = e2
    return (a2r * a1r - a2i * a1i,
            a2r * a1i + a2i * a1r,
            a2r * b1r - a2i * b1i + b2r,
            a2r * b1i + a2i * b1r + b2i)


def s5_mixer(u_raw, a_re, a_im, log_dt, b_re, b_im, c_re, c_im, d_skip, w_glu):
    b, s, _ = u_raw.shape
    u = u_raw.astype(jnp.float32).reshape(b, s, B_GROUPS, B_GROUP)
    dt = jnp.exp(log_dt.astype(jnp.float32))[:, None]
    lam_re = jnp.minimum(a_re.astype(jnp.float32), -1e-4)
    lam_im = a_im.astype(jnp.float32)
    mag = jnp.exp(lam_re * dt)
    ang = lam_im * dt
    abar_re = mag * jnp.cos(ang)
    abar_im = mag * jnp.sin(ang)
    nr = abar_re - 1.0
    ni = abar_im
    den = lam_re * lam_re + lam_im * lam_im
    z_re = (nr * lam_re + ni * lam_im) / den
    z_im = (ni * lam_re - nr * lam_im) / den
    br = b_re.astype(jnp.float32)
    bi = b_im.astype(jnp.float32)
    bbar_re = z_re[..., None] * br - z_im[..., None] * bi
    bbar_im = z_re[..., None] * bi + z_im[..., None] * br
    ut = jnp.swapaxes(u, 0, 1)
    bu_re = jnp.einsum('sbgh,gph->sbgp', ut, bbar_re)
    bu_im = jnp.einsum('sbgh,gph->sbgp', ut, bbar_im)
    at_re = jnp.broadcast_to(abar_re, (s, 1, B_GROUPS, B_STATE))
    at_im = jnp.broadcast_to(abar_im, (s, 1, B_GROUPS, B_STATE))
    _, _, x_re, x_im = lax.associative_scan(
        _complex_affine_combine, (at_re, at_im, bu_re, bu_im), axis=0)
    y = jnp.einsum('sbgp,ghp->bsgh', x_re, c_re.astype(jnp.float32)) - \
        jnp.einsum('sbgp,ghp->bsgh', x_im, c_im.astype(jnp.float32)) + \
        d_skip.astype(jnp.float32) * u
    y = jax.nn.gelu(y.reshape(b, s, B_WIDTH), approximate=False)
    return y * jax.nn.sigmoid(y @ w_glu.astype(jnp.float32))


def ab_layer(h, w_in, lb, a_re, a_im, log_dt, b_re, b_im, c_re, c_im, d_skip, w_glu, w_out):
    z = h @ w_in
    q_a = z[..., :A_WIDTH]
    f_a = z[..., A_WIDTH:2 * A_WIDTH]
    i_a = z[..., 2 * A_WIDTH:3 * A_WIDTH]
    g_a = z[..., 3 * A_WIDTH:4 * A_WIDTH]
    u_b = z[..., 4 * A_WIDTH:]
    o_a = hgrn2_mixer(q_a, f_a, i_a, g_a, lb)
    o_b = s5_mixer(u_b, a_re, a_im, log_dt, b_re, b_im, c_re, c_im, d_skip, w_glu)
    o = jnp.concatenate([o_a, o_b], axis=-1).astype(h.dtype)
    return o @ w_out


def rotary(t):
    s, d = t.shape[1], t.shape[3]
    pos = jnp.arange(s, dtype=jnp.float32)
    inv_freq = ROPE_BASE ** (-jnp.arange(0, d, 2, dtype=jnp.float32) / d)
    ang = pos[:, None] * inv_freq[None, :]
    cos = jnp.concatenate([jnp.cos(ang), jnp.cos(ang)], -1)[None, :, None, :]
    sin = jnp.concatenate([jnp.sin(ang), jnp.sin(ang)], -1)[None, :, None, :]
    t1, t2 = t[..., : d // 2], t[..., d // 2:]
    return t * cos + jnp.concatenate([-t2, t1], -1) * sin


def retention_layer(h, w_in, w_out):
    b, s, _ = h.shape
    z = (h @ w_in).astype(jnp.float32)
    q = rotary(z[..., :C_QK].reshape(b, s, C_HEADS, C_DK))
    k = rotary(z[..., C_QK:2 * C_QK].reshape(b, s, C_HEADS, C_DK)) * (C_DK ** -0.5)
    v = z[..., 2 * C_QK:2 * C_QK + C_V].reshape(b, s, C_HEADS, C_DV)
    g = z[..., 2 * C_QK + C_V:]
    log_gamma = jnp.log(1.0 - 2.0 ** (-5.0 - jnp.arange(C_HEADS, dtype=jnp.float32)))
    idx = jnp.arange(CHUNK, dtype=jnp.float32)
    intra_decay = jnp.exp(jnp.abs(idx[:, None] - idx[None, :])[None] * log_gamma[:, None, None])
    q_decay = jnp.exp((idx + 1.0)[None, :] * log_gamma[:, None])[:, :, None]
    k_decay = jnp.exp((CHUNK - 1.0 - idx)[None, :] * log_gamma[:, None])[:, :, None]
    chunk_decay = jnp.exp(CHUNK * log_gamma)[:, None, None]

    def step(r, xs):
        qt, kt, vt = xs
        scores = jnp.einsum('bhtd,bhsd->bhts', qt, kt) * intra_decay
        o = jnp.einsum('bhts,bhse->bhte', scores, vt) + \
            jnp.einsum('bhtd,bhde->bhte', qt * q_decay, r)
        r = chunk_decay * r + jnp.einsum('bhsd,bhse->bhde', kt * k_decay, vt)
        return r, o

    r0 = jnp.zeros((b, C_HEADS, C_DK, C_DV), jnp.float32)
    _, o = lax.scan(step, r0, (to_chunks(q), to_chunks(k), to_chunks(v)))
    o = head_rms(from_chunks(o)).reshape(b, s, C_V) * jax.nn.silu(g)
    return o.astype(h.dtype) @ w_out


def peer_ffn(xn, w_q, sub_keys, u_tab, v_tab):
    b, s, d = xn.shape
    blocks = xn.reshape(-1, PEER_TOKEN_BLOCK, d)
    keys32 = sub_keys.astype(jnp.float32)

    def one_block(xb):
        t = xb.shape[0]
        q = (xb @ w_q).astype(jnp.float32).reshape(t, PEER_HEADS, 2, PEER_KEY_HALF)
        sc = jnp.einsum('thpk,hpnk->thpn', q, keys32)
        s1, i1 = lax.top_k(sc[:, :, 0], PEER_TOPK)
        s2, i2 = lax.top_k(sc[:, :, 1], PEER_TOPK)
        cand = (s1[..., :, None] + s2[..., None, :]).reshape(t, PEER_HEADS, PEER_TOPK * PEER_TOPK)
        top_s, top_c = lax.top_k(cand, PEER_TOPK)
        e1 = jnp.take_along_axis(i1, top_c // PEER_TOPK, axis=-1)
        e2 = jnp.take_along_axis(i2, top_c % PEER_TOPK, axis=-1)
        experts = (e1 * PEER_NKEYS + e2).reshape(t, PEER_HEADS * PEER_TOPK)
        gate = jax.nn.softmax(top_s, axis=-1).reshape(t, PEER_HEADS * PEER_TOPK)
        u_sel = jnp.take(u_tab, experts, axis=0)
        v_sel = jnp.take(v_tab, experts, axis=0)
        hid = jax.nn.gelu(jnp.einsum('td,ted->te', xb, u_sel).astype(jnp.float32), approximate=False)
        w = (gate * hid).astype(xb.dtype)
        return jnp.einsum('te,ted->td', w, v_sel)

    return lax.map(one_block, blocks).reshape(b, s, d)


def setup_inputs(seed: int = 0) -> dict:
    key = jax.random.key(seed)
    ks = jax.random.split(key, 24)
    f32 = jnp.float32
    nrm = lambda k, shape, sc: jax.random.normal(k, shape, f32) * sc
    return {
        "x": nrm(ks[0], (BATCH, SEQ, D_MODEL), 1.0),
        "mix_norm_g": 1.0 + nrm(ks[1], (DEPTH, D_MODEL), 0.02),
        "ab_w_in": nrm(ks[2], (N_EVEN, D_MODEL, AB_IN), D_MODEL ** -0.5),
        "a_lb_param": nrm(ks[3], (N_EVEN + 1, A_WIDTH), 0.1),
        "b_a_re": -0.5 + nrm(ks[4], (N_EVEN, B_GROUPS, B_STATE), 0.01),
        "b_a_im": jnp.pi * jnp.arange(B_STATE, dtype=f32) + nrm(ks[5], (N_EVEN, B_GROUPS, B_STATE), 0.01),
        "b_log_dt": jax.random.uniform(ks[6], (N_EVEN, B_GROUPS), f32,
                                       math.log(S5_DT_MIN), math.log(S5_DT_MAX)),
        "b_b_re": nrm(ks[7], (N_EVEN, B_GROUPS, B_STATE, B_GROUP), (2 * B_GROUP) ** -0.5),
        "b_b_im": nrm(ks[8], (N_EVEN, B_GROUPS, B_STATE, B_GROUP), (2 * B_GROUP) ** -0.5),
        "b_c_re": nrm(ks[9], (N_EVEN, B_GROUPS, B_GROUP, B_STATE), (2 * B_STATE) ** -0.5),
        "b_c_im": nrm(ks[10], (N_EVEN, B_GROUPS, B_GROUP, B_STATE), (2 * B_STATE) ** -0.5),
        "b_d": nrm(ks[11], (N_EVEN, B_GROUPS, B_GROUP), 1.0),
        "b_w_glu": nrm(ks[12], (N_EVEN, B_WIDTH, B_WIDTH), B_WIDTH ** -0.5),
        "ab_w_out": nrm(ks[13], (N_EVEN, AB_OUT, D_MODEL), AB_OUT ** -0.5),
        "c_w_in": nrm(ks[14], (N_ODD, D_MODEL, C_IN), D_MODEL ** -0.5),
        "c_w_out": nrm(ks[15], (N_ODD, C_V, D_MODEL), C_V ** -0.5),
        "ffn_norm_g": 1.0 + nrm(ks[16], (DEPTH, D_MODEL), 0.02),
        "peer_w_q": nrm(ks[17], (DEPTH, D_MODEL, PEER_HEADS * PEER_QDIM), D_MODEL ** -0.5),
        "peer_sub_keys": nrm(ks[18], (DEPTH, PEER_HEADS, 2, PEER_NKEYS, PEER_KEY_HALF), PEER_KEY_HALF ** -0.5),
        "peer_u": nrm(ks[19], (DEPTH, PEER_EXPERTS, D_MODEL), D_MODEL ** -0.5),
        "peer_v": nrm(ks[20], (DEPTH, PEER_EXPERTS, D_MODEL), (PEER_HEADS * PEER_TOPK) ** -0.5),
        "final_norm_g": 1.0 + nrm(ks[21], (D_MODEL,), 0.02),
    }


def reference(x, mix_norm_g, ab_w_in, a_lb_param, b_a_re, b_a_im, b_log_dt, b_b_re, b_b_im,
              b_c_re, b_c_im, b_d, b_w_glu, ab_w_out, c_w_in, c_w_out, ffn_norm_g,
              peer_w_q, peer_sub_keys, peer_u, peer_v, final_norm_g):
    lbs = jnp.cumsum(jax.nn.softmax(a_lb_param.astype(jnp.float32), axis=0), axis=0)
    for layer in range(DEPTH):
        j = layer // 2
        h = rmsnorm(x, mix_norm_g[layer])
        if layer % 2 == 0:
            mix = ab_layer(h, ab_w_in[j], lbs[j], b_a_re[j], b_a_im[j], b_log_dt[j], b_b_re[j],
                           b_b_im[j], b_c_re[j], b_c_im[j], b_d[j], b_w_glu[j], ab_w_out[j])
        else:
            mix = retention_layer(h, c_w_in[j], c_w_out[j])
        x = x + mix.astype(x.dtype)
        hn = rmsnorm(x, ffn_norm_g[layer])
        x = x + peer_ffn(hn, peer_w_q[layer], peer_sub_keys[layer], peer_u[layer], peer_v[layer]).astype(x.dtype)
    return rmsnorm(x, final_norm_g)
```

```python
import functools
import math

import jax
import jax.numpy as jnp
from jax import lax
from jax.experimental import pallas as pl
from jax.experimental.pallas import tpu as pltpu

EPS = 1e-6
CHUNK = 64
A_KEY = 128
B_GROUP = 16
B_STATE = 64
S5_DT_FLOOR = -1e-4
S5_STEP = 16
C_HEADS = 16
ROPE_BASE = 10000.0
RET_BLOCK = 4 * CHUNK
PEER_TOPK = 16
PEER_NKEYS = 128

LANES = 128
V7X_VMEM_LIMIT = 56 * 1024 * 1024

F32 = jnp.float32
BF16 = jnp.bfloat16

_NN = (((1,), (0,)), ((), ()))
_NT = (((1,), (1,)), ((), ()))
_TN = (((0,), (0,)), ((), ()))


def _params(sem):
    return pltpu.CompilerParams(dimension_semantics=sem, vmem_limit_bytes=V7X_VMEM_LIMIT)


def _sigmoid(x):
    return 1.0 / (1.0 + jnp.exp(-x))


def _gelu(x):
    return 0.5 * x * (1.0 + lax.erf(x * math.sqrt(0.5)))


def _norm_body(*refs, has_delta, emit_x):
    x_ref = refs[0]
    d_ref = refs[1] if has_delta else None
    g_ref = refs[1 + has_delta]
    outs = refs[2 + has_delta:]
    x = x_ref[...]
    if has_delta:
        x = x + d_ref[...]
    if emit_x:
        outs[0][...] = x
    h_ref = outs[-1]
    ms = jnp.mean(x * x, axis=-1, keepdims=True)
    h_ref[...] = (x * lax.rsqrt(ms + EPS) * g_ref[...]).astype(h_ref.dtype)


def _rmsnorm(x, g, delta=None, *, emit_x=False, out_dtype=BF16, tm=256):
    m, d = x.shape
    has_delta = delta is not None
    row = pl.BlockSpec((tm, d), lambda i: (i, 0))
    in_specs = [row] + ([row] if has_delta else []) + [pl.BlockSpec((1, d), lambda i: (0, 0))]
    args = [x] + ([delta] if has_delta else []) + [g.reshape(1, d)]
    out_shape = [jax.ShapeDtypeStruct((m, d), out_dtype)]
    out_specs = [row]
    if emit_x:
        out_shape = [jax.ShapeDtypeStruct((m, d), F32)] + out_shape
        out_specs = [row] + out_specs
    res = pl.pallas_call(
        functools.partial(_norm_body, has_delta=has_delta, emit_x=emit_x),
        grid=(m // tm,),
        in_specs=in_specs,
        out_specs=out_specs,
        out_shape=out_shape,
        compiler_params=_params(("parallel",)),
        name="rmsnorm",
    )(*args)
    return res if emit_x else res[0]


def _mm_body(*refs, nk, trans_b, epilogue):
    a_ref, b_ref = refs[0], refs[1]
    e_ref = refs[2] if epilogue else None
    o_ref = refs[2 + bool(epilogue)]
    acc_ref = refs[3 + bool(epilogue)] if nk > 1 else None
    part = lax.dot_general(a_ref[...].astype(b_ref.dtype), b_ref[...], _NT if trans_b else _NN,
                           preferred_element_type=F32)

    def finish(acc):
        if epilogue == "residual":
            acc = acc + e_ref[...]
        elif epilogue == "glu":
            acc = e_ref[...].astype(F32) * _sigmoid(acc)
        o_ref[...] = acc.astype(o_ref.dtype)

    if nk == 1:
        finish(part)
        return
    k = pl.program_id(2)

    @pl.when(k == 0)
    def _():
        acc_ref[...] = part

    @pl.when(k > 0)
    def _():
        acc_ref[...] += part

    @pl.when(k == nk - 1)
    def _():
        finish(acc_ref[...])


def _matmul(a, b, *, trans_b=False, extra=None, epilogue=None, out_dtype=F32,
            tm=512, tn=1024, tk=2048, name="matmul"):
    m, kdim = a.shape
    n = b.shape[0] if trans_b else b.shape[1]
    tm, tn, tk = math.gcd(tm, m), math.gcd(tn, n), math.gcd(tk, kdim)
    nk = kdim // tk
    a_spec = pl.BlockSpec((tm, tk), lambda i, j, k: (i, k))
    if trans_b:
        b_spec = pl.BlockSpec((tn, tk), lambda i, j, k: (j, k))
    else:
        b_spec = pl.BlockSpec((tk, tn), lambda i, j, k: (k, j))
    o_spec = pl.BlockSpec((tm, tn), lambda i, j, k: (i, j))
    in_specs = [a_spec, b_spec] + ([o_spec] if epilogue else [])
    args = [a, b] + ([extra] if epilogue else [])
    scratch = [pltpu.VMEM((tm, tn), F32)] if nk > 1 else []
    return pl.pallas_call(
        functools.partial(_mm_body, nk=nk, trans_b=trans_b, epilogue=epilogue),
        grid=(m // tm, n // tn, nk),
        in_specs=in_specs,
        out_specs=o_spec,
        out_shape=jax.ShapeDtypeStruct((m, n), out_dtype),
        scratch_shapes=scratch,
        compiler_params=_params(("parallel", "parallel", "arbitrary")),
        name=name,
    )(*args)


HG_SUB = 16


def _hgrn2_body(lbp_ref, q_ref, f_ref, i_ref, g_ref, o_ref, st_ref, *, n_chunks, slot):
    @pl.when(pl.program_id(2) == 0)
    def _():
        st_ref[...] = jnp.zeros_like(st_ref)

    p = lbp_ref[...]
    ex = jnp.exp(p - jnp.max(p, axis=0, keepdims=True))
    sm = ex / jnp.sum(ex, axis=0, keepdims=True)
    lb = jnp.sum(sm[: slot + 1], axis=0, keepdims=True)

    ri = lax.broadcasted_iota(jnp.int32, (CHUNK, CHUNK), 0)
    ci = lax.broadcasted_iota(jnp.int32, (CHUNK, CHUNK), 1)
    tri = (ci <= ri).astype(F32)
    sub_row = lax.broadcasted_iota(jnp.int32, (HG_SUB, 1), 0)
    n_sub = CHUNK // HG_SUB

    def chunk(ci_, carry):
        r0 = pl.multiple_of(ci_ * CHUNK, CHUNK)
        rows = pl.ds(r0, CHUNK)
        f = lb + (1.0 - lb) * _sigmoid(f_ref[rows, :])
        qr = q_ref[rows, :]
        q = qr * _sigmoid(qr)
        k = 1.0 - f
        v = i_ref[rows, :]
        cum = jnp.dot(tri, jnp.log(f), precision=lax.Precision.HIGHEST,
                      preferred_element_type=F32)
        v16 = v.astype(BF16)
        parts = []
        for sb in range(n_sub):
            lo = sb * HG_SUB
            cs = cum[lo:lo + HG_SUB]
            qs = q[lo:lo + HG_SUB]
            ks = k[lo:lo + HG_SUB]
            vs = v[lo:lo + HG_SUB]
            acc = jnp.zeros((HG_SUB, A_KEY), F32)
            for s in range(HG_SUB):
                diff = jnp.where(sub_row >= s, cs - cs[s:s + 1], -jnp.inf)
                a = jnp.sum(qs * jnp.exp(diff) * ks[s:s + 1], axis=-1, keepdims=True)
                acc = acc + a * vs[s:s + 1]
            if sb > 0:
                ref = cum[lo - 1:lo]
                qx = (qs * jnp.exp(cs - ref)).astype(BF16)
                kx = (k[:lo] * jnp.exp(ref - cum[:lo])).astype(BF16)
                att = lax.dot_general(qx, kx, _NT, preferred_element_type=F32)
                acc = acc + jnp.dot(att.astype(BF16), v16[:lo], preferred_element_type=F32)
            parts.append(acc)
        o = jnp.concatenate(parts, axis=0)
        st = st_ref[...]
        last = cum[CHUNK - 1:CHUNK]
        o = o + lax.dot_general((q * jnp.exp(cum)).astype(BF16), st.astype(BF16), _NT,
                                preferred_element_type=F32)
        kd = (k * jnp.exp(last - cum)).astype(BF16)
        st_ref[...] = st * jnp.exp(last) + lax.dot_general(v16, kd, _TN,
                                                           preferred_element_type=F32)
        o = o * lax.rsqrt(jnp.mean(o * o, axis=-1, keepdims=True) + EPS)
        gr = g_ref[rows, :]
        o_ref[rows, :] = (o * gr * _sigmoid(gr)).astype(o_ref.dtype)
        return carry

    lax.fori_loop(0, n_chunks, chunk, 0)


def _hgrn2(z, lb_param, slot, batch, seq, width, *, rows=256):
    heads = width // A_KEY
    nblk = seq // rows
    lane = lambda off: pl.BlockSpec(
        (rows, A_KEY), lambda b, h, c, off=off: (b * nblk + c, off * heads + h))
    n_slots = lb_param.shape[0]
    return pl.pallas_call(
        functools.partial(_hgrn2_body, n_chunks=rows // CHUNK, slot=slot),
        grid=(batch, heads, nblk),
        in_specs=[pl.BlockSpec((n_slots, A_KEY), lambda b, h, c: (0, h)),
                  lane(0), lane(1), lane(2), lane(3)],
        out_specs=pl.BlockSpec((rows, A_KEY), lambda b, h, c: (b * nblk + c, h)),
        out_shape=jax.ShapeDtypeStruct((batch * seq, width), BF16),
        scratch_shapes=[pltpu.VMEM((A_KEY, A_KEY), F32)],
        compiler_params=_params(("parallel", "parallel", "arbitrary")),
        name="hgrn2",
    )(lb_param, z, z, z, z)


def _s5_tables(a_re, a_im, log_dt, b_re, b_im, c_re, c_im):
    hi = lax.Precision.HIGHEST
    L = S5_STEP
    groups, states = a_re.shape
    tiles = groups * B_GROUP // LANES
    gpt = LANES // B_GROUP
    dt = jnp.exp(log_dt.astype(F32))[:, None]
    lam_re = jnp.minimum(a_re.astype(F32), S5_DT_FLOOR)
    lam_im = a_im.astype(F32)
    mag = jnp.exp(lam_re * dt)
    ang = lam_im * dt
    ab_re = mag * jnp.cos(ang)
    ab_im = mag * jnp.sin(ang)
    nr = ab_re - 1.0
    ni = ab_im
    den = lam_re * lam_re + lam_im * lam_im
    z_re = (nr * lam_re + ni * lam_im) / den
    z_im = (ni * lam_re - nr * lam_im) / den
    br = b_re.astype(F32)
    bi = b_im.astype(F32)
    bb_re = z_re[..., None] * br - z_im[..., None] * bi
    bb_im = z_re[..., None] * bi + z_im[..., None] * br
    pw_re, pw_im = [jnp.ones_like(ab_re)], [jnp.zeros_like(ab_re)]
    for _ in range(L):
        pr, pi = pw_re[-1], pw_im[-1]
        pw_re.append(pr * ab_re - pi * ab_im)
        pw_im.append(pr * ab_im + pi * ab_re)
    pw_re = jnp.stack(pw_re)
    pw_im = jnp.stack(pw_im)
    cr = c_re.astype(F32)[None]
    ci = c_im.astype(F32)[None]
    cp_re = cr * pw_re[:, :, None, :] - ci * pw_im[:, :, None, :]
    cp_im = cr * pw_im[:, :, None, :] + ci * pw_re[:, :, None, :]
    lag = (jnp.einsum("lghp,gpk->lghk", cp_re[:L], bb_re, precision=hi)
           - jnp.einsum("lghp,gpk->lghk", cp_im[:L], bb_im, precision=hi))
    eye = jnp.eye(gpt, dtype=F32)
    lag = lag.reshape(L, tiles, gpt, B_GROUP, B_GROUP)
    kcat = jnp.einsum("ljghk,gm->jgklmh", lag, eye).reshape(tiles, LANES, L * LANES)
    rev = slice(L - 1, None, -1)
    wb_re = pw_re[rev][..., None] * bb_re[None] - pw_im[rev][..., None] * bb_im[None]
    wb_im = pw_re[rev][..., None] * bb_im[None] + pw_im[rev][..., None] * bb_re[None]
    wb = jnp.stack([wb_re, wb_im], axis=1)
    wb = wb.reshape(L, 2, tiles, gpt, states, B_GROUP)
    wst = jnp.einsum("srjgpk,gm->jsgkrmp", wb, eye).reshape(
        tiles, L, LANES, 2 * gpt * states)
    mo = jnp.stack([cp_re[1:], -cp_im[1:]], axis=1)
    mo = mo.reshape(L, 2, tiles, gpt, B_GROUP, states)
    mst = jnp.einsum("trjghp,gm->jrgptmh", mo, eye).reshape(
        tiles, 2 * gpt * states, L * LANES)
    al = jnp.stack([pw_re[L].reshape(tiles, gpt * states),
                    pw_im[L].reshape(tiles, gpt * states)], axis=1)
    return kcat.astype(BF16), wst.astype(BF16), mst.astype(BF16), al


def _s5_body(u_ref, kc_ref, ws_ref, ms_ref, al_ref, d_ref, o_ref, sl_ref, x_ref, y_ref, *, nc):
    L = S5_STEP
    half = sl_ref.shape[1] // 2
    step_rows = lambda s: u_ref[pl.ds(s, nc, stride=L), :]
    for s in range(L):
        part = jnp.dot(step_rows(s).astype(BF16), ws_ref[s], preferred_element_type=F32)
        if s == 0:
            sl_ref[...] = part
        else:
            sl_ref[...] += part
    alr = al_ref[0:1, :]
    ali = al_ref[1:2, :]

    def step(c, carry):
        xr, xi = carry
        row = pl.ds(c, 1)
        x_ref[row, 0:half] = xr
        x_ref[row, half:] = xi
        sr = sl_ref[row, 0:half]
        si = sl_ref[row, half:]
        return alr * xr - ali * xi + sr, alr * xi + ali * xr + si

    zero = jnp.zeros((1, half), F32)
    lax.fori_loop(0, nc, step, (zero, zero))
    y_ref[...] = jnp.dot(x_ref[...].astype(BF16), ms_ref[...], preferred_element_type=F32)
    for s in range(L):
        y_ref[:, s * LANES:] += jnp.dot(step_rows(s).astype(BF16),
                                        kc_ref[:, :(L - s) * LANES],
                                        preferred_element_type=F32)
    for t in range(L):
        y = y_ref[:, t * LANES:(t + 1) * LANES] + d_ref[...] * step_rows(t)
        o_ref[pl.ds(t, nc, stride=L), :] = _gelu(y)


def _s5(z, col_off, tables, d_skip, batch, seq, width):
    kcat, wst, mst, al = tables
    L = S5_STEP
    tiles = width // LANES
    nc = seq // L
    nstate = wst.shape[-1]
    off = col_off // LANES
    return pl.pallas_call(
        functools.partial(_s5_body, nc=nc),
        grid=(tiles, batch),
        in_specs=[
            pl.BlockSpec((seq, LANES), lambda j, b: (b, off + j)),
            pl.BlockSpec((None, LANES, L * LANES), lambda j, b: (j, 0, 0)),
            pl.BlockSpec((None, L, LANES, nstate), lambda j, b: (j, 0, 0, 0)),
            pl.BlockSpec((None, nstate, L * LANES), lambda j, b: (j, 0, 0)),
            pl.BlockSpec((None, 2, nstate // 2), lambda j, b: (j, 0, 0)),
            pl.BlockSpec((None, 1, LANES), lambda j, b: (j, 0, 0)),
        ],
        out_specs=pl.BlockSpec((seq, LANES), lambda j, b: (b, j)),
        out_shape=jax.ShapeDtypeStruct((batch * seq, width), F32),
        scratch_shapes=[pltpu.VMEM((nc, nstate), F32), pltpu.VMEM((nc, nstate), F32),
                        pltpu.VMEM((nc, L * LANES), F32)],
        compiler_params=_params(("parallel", "arbitrary")),
        name="s5",
    )(z, kcat, wst, mst, al, d_skip.astype(F32).reshape(tiles, 1, LANES))


def _ret_body(lg_ref, q_ref, k_ref, v_ref, g_ref, cos_ref, sin_ref, o_ref, r_ref, *, dk):
    @pl.when(pl.program_id(2) == 0)
    def _():
        r_ref[...] = jnp.zeros_like(r_ref)

    n = RET_BLOCK
    lg = lg_ref[pl.program_id(1)]
    cos = cos_ref[...]
    sin = sin_ref[...]
    hd = dk // 2

    def rot(t):
        t1, t2 = t[:, :hd], t[:, hd:]
        return jnp.concatenate([t1 * cos - t2 * sin, t2 * cos + t1 * sin], axis=1)

    q = rot(q_ref[...].astype(F32))
    k = rot(k_ref[...].astype(F32)) * (dk ** -0.5)
    ti = lax.broadcasted_iota(jnp.int32, (n, n), 0)
    si = lax.broadcasted_iota(jnp.int32, (n, n), 1)
    dist = jnp.abs(ti - si).astype(F32)
    decay = jnp.where(si // CHUNK <= ti // CHUNK, jnp.exp(dist * lg), 0.0)
    sc = lax.dot_general(q.astype(BF16), k.astype(BF16), _NT, preferred_element_type=F32) * decay
    v = v_ref[...]
    pos = lax.broadcasted_iota(jnp.int32, (n, 1), 0).astype(F32)
    qd = jnp.exp((pos + 1.0) * lg)
    kd = jnp.exp((n - 1.0 - pos) * lg)
    r = r_ref[...]
    o = (jnp.dot(sc.astype(BF16), v, preferred_element_type=F32)
         + jnp.dot((q * qd).astype(BF16), r.astype(BF16), preferred_element_type=F32))
    r_ref[...] = jnp.exp(n * lg) * r + lax.dot_general((k * kd).astype(BF16), v, _TN,
                                                       preferred_element_type=F32)
    o = o * lax.rsqrt(jnp.mean(o * o, axis=-1, keepdims=True) + EPS)
    g = g_ref[...].astype(F32)
    o_ref[...] = (o * g * _sigmoid(g)).astype(o_ref.dtype)


def _retention(z, batch, seq, d_qk, d_v):
    dk = d_qk // C_HEADS
    dv = d_v // C_HEADS
    n = RET_BLOCK
    nblk = seq // n
    pos = jnp.arange(seq, dtype=F32)
    inv_freq = ROPE_BASE ** (-jnp.arange(0, dk, 2, dtype=F32) / dk)
    ang = pos[:, None] * inv_freq[None, :]
    log_gamma = jnp.log(1.0 - 2.0 ** (-5.0 - jnp.arange(C_HEADS, dtype=F32)))
    row = lambda b, h, c, lg: b * nblk + c
    grid_spec = pltpu.PrefetchScalarGridSpec(
        num_scalar_prefetch=1,
        grid=(batch, C_HEADS, nblk),
        in_specs=[
            pl.BlockSpec((n, dk), lambda b, h, c, lg: (row(b, h, c, lg), h)),
            pl.BlockSpec((n, dk), lambda b, h, c, lg: (row(b, h, c, lg), C_HEADS + h)),
            pl.BlockSpec((n, dv), lambda b, h, c, lg: (row(b, h, c, lg), 2 * d_qk // dv + h)),
            pl.BlockSpec((n, dv), lambda b, h, c, lg: (row(b, h, c, lg),
                                                       (2 * d_qk + d_v) // dv + h)),
            pl.BlockSpec((n, dk // 2), lambda b, h, c, lg: (c, 0)),
            pl.BlockSpec((n, dk // 2), lambda b, h, c, lg: (c, 0)),
        ],
        out_specs=pl.BlockSpec((n, dv), lambda b, h, c, lg: (row(b, h, c, lg), h)),
        scratch_shapes=[pltpu.VMEM((dk, dv), F32)],
    )
    return pl.pallas_call(
        functools.partial(_ret_body, dk=dk),
        grid_spec=grid_spec,
        out_shape=jax.ShapeDtypeStruct((batch * seq, d_v), BF16),
        compiler_params=_params(("parallel", "parallel", "arbitrary")),
        name="retention",
    )(log_gamma, z, z, z, z, jnp.cos(ang), jnp.sin(ang))


def _fold_body(k_ref, w_ref, o_ref):
    o_ref[...] = lax.dot_general(k_ref[...].astype(BF16), w_ref[...].astype(BF16), _NT,
                                 preferred_element_type=F32).astype(o_ref.dtype)


def _peer_fold(sub_keys, w_q, *, td=1024):
    heads, two, nk, kh = sub_keys.shape
    d = w_q.shape[0]
    td = min(td, d)
    return pl.pallas_call(
        _fold_body,
        grid=(heads * two, d // td),
        in_specs=[pl.BlockSpec((nk, kh), lambda hp, j: (hp, 0)),
                  pl.BlockSpec((td, kh), lambda hp, j: (j, hp))],
        out_specs=pl.BlockSpec((nk, td), lambda hp, j: (hp, j)),
        out_shape=jax.ShapeDtypeStruct((heads * two * nk, d), BF16),
        compiler_params=_params(("parallel", "parallel")),
        name="peer_fold",
    )(sub_keys.reshape(heads * two * nk, kh), w_q)


def _top_values(s, k):
    rows = lax.broadcasted_iota(jnp.int32, s.shape, 0).astype(F32)
    big = float(s.shape[0])
    vals = []
    for _ in range(k):
        m = jnp.max(s, axis=0, keepdims=True)
        first = jnp.min(jnp.where(s == m, rows, big), axis=0, keepdims=True)
        vals.append(m)
        s = jnp.where(rows == first, -jnp.inf, s)
    return jnp.concatenate(vals, axis=0)


def _peer_topk_body(sc_ref, ea_ref, tau_ref):
    nk = PEER_NKEYS
    k = PEER_TOPK
    s1 = sc_ref[0:nk, :]
    s2 = sc_ref[nk:2 * nk, :]
    t1 = _top_values(s1, k)
    t2 = _top_values(s2, k)
    cand = jnp.concatenate([t1[a:a + 1] + t2 for a in range(k)], axis=0)
    top = _top_values(cand, k)
    z = jnp.sum(jnp.exp(top - top[0:1]), axis=0, keepdims=True)
    ea_ref[0:nk, :] = jnp.exp(s1 - t1[0:1]) / z
    ea_ref[nk:2 * nk, :] = jnp.exp(s2 - t2[0:1])
    tau_ref[...] = jnp.broadcast_to(top[k - 1:k], tau_ref.shape)


def _peer_topk(sc_t, heads, *, tt=128):
    rows, t = sc_t.shape
    per = rows // heads
    return pl.pallas_call(
        _peer_topk_body,
        grid=(t // tt, heads),
        in_specs=[pl.BlockSpec((per, tt), lambda i, h: (h, i))],
        out_specs=[pl.BlockSpec((per, tt), lambda i, h: (h, i)),
                   pl.BlockSpec((None, 8, tt), lambda i, h: (h, 0, i))],
        out_shape=[jax.ShapeDtypeStruct((rows, t), F32),
                   jax.ShapeDtypeStruct((heads, 8, t), F32)],
        compiler_params=_params(("parallel", "parallel")),
        name="peer_topk",
    )(sc_t)


def _peer_main_body(xn_ref, u_ref, v_ref, sc_ref, ea_ref, tau_ref, o_ref,
                    p_ref, act_ref, s1_ref, a1_ref, *, heads):
    j = pl.program_id(1)

    @pl.when(j == 0)
    def _():
        o_ref[...] = jnp.zeros_like(o_ref)

    nk = PEER_NKEYS
    te, tt = p_ref.shape
    n_e1 = te // nk
    xn = xn_ref[...]
    sub = lax.broadcasted_iota(jnp.int32, (8, 1), 0)

    def key_row(ref, row):
        grp = ref[pl.ds(pl.multiple_of((row // 8) * 8, 8), 8), :]
        return jnp.sum(jnp.where(sub == row % 8, grp, 0.0), axis=0, keepdims=True)

    for r in range(n_e1):
        e1 = j * n_e1 + r
        rows = slice(r * nk, (r + 1) * nk)
        act_ref[...] = _gelu(lax.dot_general(u_ref[rows, :], xn, _NT,
                                             preferred_element_type=F32))
        for h in range(heads):
            s1_ref[h:h + 1, :] = key_row(sc_ref, h * 2 * nk + e1)
            a1_ref[h:h + 1, :] = key_row(ea_ref, h * 2 * nk + e1)

        def lane_chunk(c, carry):
            lanes = pl.ds(pl.multiple_of(c * LANES, LANES), LANES)
            w = jnp.zeros((nk, LANES), F32)
            for h in range(heads):
                base = h * 2 * nk
                s2 = sc_ref[base + nk:base + 2 * nk, lanes]
                a2 = ea_ref[base + nk:base + 2 * nk, lanes]
                hit = s1_ref[h:h + 1, lanes] + s2 >= tau_ref[h, 0:1, lanes]
                w = w + jnp.where(hit, a1_ref[h:h + 1, lanes] * a2, 0.0)
            p_ref[rows, lanes] = (act_ref[:, lanes] * w).astype(p_ref.dtype)
            return carry

        lax.fori_loop(0, tt // LANES, lane_chunk, 0)
    o_ref[...] += lax.dot_general(p_ref[...], v_ref[...], _TN, preferred_element_type=F32)


def _peer_main(xn, u16, v16, sc_t, ea_t, tau, heads, *, tt=512, te=512):
    t, d = xn.shape
    n_exp = u16.shape[0]
    tt = min(tt, t)
    rows = sc_t.shape[0]
    once = dict(pipeline_mode=pl.Buffered(1))
    return pl.pallas_call(
        functools.partial(_peer_main_body, heads=heads),
        grid=(t // tt, n_exp // te),
        in_specs=[
            pl.BlockSpec((tt, d), lambda i, j: (i, 0), **once),
            pl.BlockSpec((te, d), lambda i, j: (j, 0)),
            pl.BlockSpec((te, d), lambda i, j: (j, 0)),
            pl.BlockSpec((rows, tt), lambda i, j: (0, i), **once),
            pl.BlockSpec((rows, tt), lambda i, j: (0, i), **once),
            pl.BlockSpec((heads, 8, tt), lambda i, j: (0, 0, i), **once),
        ],
        out_specs=pl.BlockSpec((tt, d), lambda i, j: (i, 0)),
        out_shape=jax.ShapeDtypeStruct((t, d), F32),
        scratch_shapes=[pltpu.VMEM((te, tt), BF16), pltpu.VMEM((PEER_NKEYS, tt), F32),
                        pltpu.VMEM((heads, tt), F32), pltpu.VMEM((heads, tt), F32)],
        compiler_params=_params(("parallel", "arbitrary")),
        name="peer_main",
    )(xn, u16, v16, sc_t, ea_t, tau)


def _peer(xn, w_q, sub_keys, u_tab, v_tab):
    heads = sub_keys.shape[0]
    fold = _peer_fold(sub_keys, w_q)
    sc_t = _matmul(fold, xn, trans_b=True, name="peer_scores")
    ea_t, tau = _peer_topk(sc_t, heads)
    return _peer_main(xn, u_tab.astype(BF16), v_tab.astype(BF16), sc_t, ea_t, tau, heads)


def kernel(x, mix_norm_g, ab_w_in, a_lb_param, b_a_re, b_a_im, b_log_dt, b_b_re, b_b_im, b_c_re, b_c_im, b_d, b_w_glu, ab_w_out, c_w_in, c_w_out, ffn_norm_g, peer_w_q, peer_sub_keys, peer_u, peer_v, final_norm_g):
    batch, seq, d = x.shape
    depth = mix_norm_g.shape[0]
    xs = x.reshape(batch * seq, d)
    delta = None
    for layer in range(depth):
        j = layer // 2
        if delta is None:
            h = _rmsnorm(xs, mix_norm_g[layer])
        else:
            xs, h = _rmsnorm(xs, mix_norm_g[layer], delta, emit_x=True)
        if layer % 2 == 0:
            a_width = a_lb_param.shape[1]
            b_width = b_w_glu.shape[1]
            z = _matmul(h, ab_w_in[j].astype(BF16), name="ab_in")
            o_a = _hgrn2(z, a_lb_param.astype(F32), j, batch, seq, a_width)
            tables = _s5_tables(b_a_re[j], b_a_im[j], b_log_dt[j], b_b_re[j], b_b_im[j],
                                b_c_re[j], b_c_im[j])
            yg = _s5(z, 4 * a_width, tables, b_d[j], batch, seq, b_width)
            o_b = _matmul(yg, b_w_glu[j].astype(BF16), extra=yg, epilogue="glu",
                          out_dtype=BF16, name="s5_glu")
            o = jnp.concatenate([o_a, o_b], axis=-1)
            xs = _matmul(o, ab_w_out[j].astype(BF16), extra=xs, epilogue="residual",
                         name="ab_out")
        else:
            d_v = c_w_out.shape[1]
            d_qk = (c_w_in.shape[2] - 2 * d_v) // 2
            z = _matmul(h, c_w_in[j].astype(BF16), out_dtype=BF16, name="ret_in")
            o = _retention(z, batch, seq, d_qk, d_v)
            xs = _matmul(o, c_w_out[j].astype(BF16), extra=xs, epilogue="residual",
                         name="ret_out")
        hn = _rmsnorm(xs, ffn_norm_g[layer])
        delta = _peer(hn, peer_w_q[layer], peer_sub_keys[layer], peer_u[layer], peer_v[layer])
    out = _rmsnorm(xs, final_norm_g, delta, out_dtype=F32)
    return out.reshape(batch, seq, d)
```

```python
import functools
import math

import jax
import jax.numpy as jnp
from jax import lax
from jax.experimental import pallas as pl
from jax.experimental.pallas import tpu as pltpu

EPS = 1e-6
CHUNK = 64
A_KEY = 128
B_GROUP = 16
B_STATE = 64
S5_DT_FLOOR = -1e-4
S5_STEP = 16
C_HEADS = 16
ROPE_BASE = 10000.0
RET_BLOCK = 4 * CHUNK
PEER_TOPK = 16
PEER_NKEYS = 128

LANES = 128
SUBLANES = 8
V7X_VMEM_LIMIT = 56 * 1024 * 1024

F32 = jnp.float32
BF16 = jnp.bfloat16

_NN = (((1,), (0,)), ((), ()))
_NT = (((1,), (1,)), ((), ()))
_TN = (((0,), (0,)), ((), ()))


def _params(sem):
    return pltpu.CompilerParams(dimension_semantics=sem, vmem_limit_bytes=V7X_VMEM_LIMIT)


def _sigmoid(x):
    return 1.0 / (1.0 + jnp.exp(-x))


def _gelu(x):
    return 0.5 * x * (1.0 + lax.erf(x * math.sqrt(0.5)))


def _norm_body(*refs, has_delta, emit_x):
    x_ref = refs[0]
    d_ref = refs[1] if has_delta else None
    g_ref = refs[1 + has_delta]
    outs = refs[2 + has_delta:]
    x = x_ref[...]
    if has_delta:
        x = x + d_ref[...]
    if emit_x:
        outs[0][...] = x
    h_ref = outs[-1]
    ms = jnp.mean(x * x, axis=-1, keepdims=True)
    h_ref[...] = (x * lax.rsqrt(ms + EPS) * g_ref[...]).astype(h_ref.dtype)


def _rmsnorm(x, g, delta=None, *, emit_x=False, out_dtype=BF16, tm=256):
    m, d = x.shape
    has_delta = delta is not None
    row = pl.BlockSpec((tm, d), lambda i: (i, 0))
    in_specs = [row] + ([row] if has_delta else []) + [pl.BlockSpec((1, d), lambda i: (0, 0))]
    args = [x] + ([delta] if has_delta else []) + [g.reshape(1, d)]
    out_shape = [jax.ShapeDtypeStruct((m, d), out_dtype)]
    out_specs = [row]
    if emit_x:
        out_shape = [jax.ShapeDtypeStruct((m, d), F32)] + out_shape
        out_specs = [row] + out_specs
    res = pl.pallas_call(
        functools.partial(_norm_body, has_delta=has_delta, emit_x=emit_x),
        grid=(m // tm,),
        in_specs=in_specs,
        out_specs=out_specs,
        out_shape=out_shape,
        compiler_params=_params(("parallel",)),
        name="rmsnorm",
    )(*args)
    return res if emit_x else res[0]


def _mm_body(*refs, nk, trans_b, epilogue):
    a_ref, b_ref = refs[0], refs[1]
    e_ref = refs[2] if epilogue else None
    o_ref = refs[2 + bool(epilogue)]
    acc_ref = refs[3 + bool(epilogue)] if nk > 1 else None
    part = lax.dot_general(a_ref[...].astype(b_ref.dtype), b_ref[...], _NT if trans_b else _NN,
                           preferred_element_type=F32)

    def finish(acc):
        if epilogue == "residual":
            acc = acc + e_ref[...]
        elif epilogue == "glu":
            acc = e_ref[...].astype(F32) * _sigmoid(acc)
        o_ref[...] = acc.astype(o_ref.dtype)

    if nk == 1:
        finish(part)
        return
    k = pl.program_id(2)

    @pl.when(k == 0)
    def _():
        acc_ref[...] = part

    @pl.when(k > 0)
    def _():
        acc_ref[...] += part

    @pl.when(k == nk - 1)
    def _():
        finish(acc_ref[...])


def _matmul(a, b, *, trans_b=False, extra=None, epilogue=None, out_dtype=F32,
            tm=1024, tn=1024, tk=4096, name="matmul"):
    m, kdim = a.shape
    n = b.shape[0] if trans_b else b.shape[1]
    tm, tn, tk = math.gcd(tm, m), math.gcd(tn, n), math.gcd(tk, kdim)
    nk = kdim // tk
    a_spec = pl.BlockSpec((tm, tk), lambda i, j, k: (i, k))
    if trans_b:
        b_spec = pl.BlockSpec((tn, tk), lambda i, j, k: (j, k))
    else:
        b_spec = pl.BlockSpec((tk, tn), lambda i, j, k: (k, j))
    o_spec = pl.BlockSpec((tm, tn), lambda i, j, k: (i, j))
    in_specs = [a_spec, b_spec] + ([o_spec] if epilogue else [])
    args = [a, b] + ([extra] if epilogue else [])
    scratch = [pltpu.VMEM((tm, tn), F32)] if nk > 1 else []
    return pl.pallas_call(
        functools.partial(_mm_body, nk=nk, trans_b=trans_b, epilogue=epilogue),
        grid=(m // tm, n // tn, nk),
        in_specs=in_specs,
        out_specs=o_spec,
        out_shape=jax.ShapeDtypeStruct((m, n), out_dtype),
        scratch_shapes=scratch,
        compiler_params=_params(("parallel", "parallel", "arbitrary")),
        name=name,
    )(*args)


HG_SUB = 16


def _hgrn2_body(lbp_ref, q_ref, f_ref, i_ref, g_ref, o_ref, st_ref, *, n_chunks, slot, hpb):
    @pl.when(pl.program_id(2) == 0)
    def _():
        st_ref[...] = jnp.zeros_like(st_ref)

    p = lbp_ref[...]
    ex = jnp.exp(p - jnp.max(p, axis=0, keepdims=True))
    sm = ex / jnp.sum(ex, axis=0, keepdims=True)
    lb = jnp.sum(sm[: slot + 1], axis=0, keepdims=True)

    ri = lax.broadcasted_iota(jnp.int32, (CHUNK, CHUNK), 0)
    ci = lax.broadcasted_iota(jnp.int32, (CHUNK, CHUNK), 1)
    tri = (ci <= ri).astype(F32)
    sub_row = lax.broadcasted_iota(jnp.int32, (HG_SUB, 1), 0)
    n_sub = CHUNK // HG_SUB
    heads_cols = [slice(hh * A_KEY, (hh + 1) * A_KEY) for hh in range(hpb)]

    def per_head(fn):
        return jnp.concatenate([fn(hh, cols) for hh, cols in enumerate(heads_cols)], axis=1)

    def head_sum(x):
        return per_head(lambda hh, cols: jnp.broadcast_to(
            jnp.sum(x[:, cols], axis=-1, keepdims=True), (x.shape[0], A_KEY)))

    def chunk(ci_, carry):
        rows = pl.ds(pl.multiple_of(ci_ * CHUNK, CHUNK), CHUNK)
        f = lb + (1.0 - lb) * _sigmoid(f_ref[rows, :])
        qr = q_ref[rows, :]
        q = qr * _sigmoid(qr)
        k = 1.0 - f
        v = i_ref[rows, :]
        cum = jnp.dot(tri, jnp.log(f), precision=lax.Precision.HIGHEST,
                      preferred_element_type=F32)
        v16 = v.astype(BF16)
        parts = []
        for sb in range(n_sub):
            lo = sb * HG_SUB
            cs = cum[lo:lo + HG_SUB]
            qs = q[lo:lo + HG_SUB]
            ks = k[lo:lo + HG_SUB]
            vs = v[lo:lo + HG_SUB]
            acc = jnp.zeros(cs.shape, F32)
            for s in range(HG_SUB):
                diff = jnp.where(sub_row >= s, cs - cs[s:s + 1], -jnp.inf)
                acc = acc + head_sum(qs * jnp.exp(diff) * ks[s:s + 1]) * vs[s:s + 1]
            if sb > 0:
                ref = cum[lo - 1:lo]
                qx = (qs * jnp.exp(cs - ref)).astype(BF16)
                kx = (k[:lo] * jnp.exp(ref - cum[:lo])).astype(BF16)

                def off_diag(hh, cols):
                    att = lax.dot_general(qx[:, cols], kx[:, cols], _NT,
                                          preferred_element_type=F32)
                    return jnp.dot(att.astype(BF16), v16[:lo, cols], preferred_element_type=F32)

                acc = acc + per_head(off_diag)
            parts.append(acc)
        o = jnp.concatenate(parts, axis=0)
        last = cum[CHUNK - 1:CHUNK]
        qd = (q * jnp.exp(cum)).astype(BF16)
        kd = (k * jnp.exp(last - cum)).astype(BF16)
        decay = jnp.exp(last)
        states = [st_ref[hh] for hh in range(hpb)]
        o = o + per_head(lambda hh, cols: lax.dot_general(
            qd[:, cols], states[hh].astype(BF16), _NT, preferred_element_type=F32))
        for hh, cols in enumerate(heads_cols):
            st_ref[hh] = states[hh] * decay[:, cols] + lax.dot_general(
                v16[:, cols], kd[:, cols], _TN, preferred_element_type=F32)
        o = o * lax.rsqrt(head_sum(o * o) * (1.0 / A_KEY) + EPS)
        gr = g_ref[rows, :]
        o_ref[rows, :] = (o * gr * _sigmoid(gr)).astype(o_ref.dtype)
        return carry

    lax.fori_loop(0, n_chunks, chunk, 0)


def _hgrn2(z, lb_param, slot, batch, seq, width, *, rows=512, heads_per_block=8):
    heads = width // A_KEY
    hpb = math.gcd(heads_per_block, heads)
    hblocks = heads // hpb
    rows = math.gcd(rows, seq)
    nblk = seq // rows
    lane = lambda off: pl.BlockSpec(
        (rows, hpb * A_KEY), lambda b, h, c, off=off: (b * nblk + c, off * hblocks + h))
    n_slots = lb_param.shape[0]
    return pl.pallas_call(
        functools.partial(_hgrn2_body, n_chunks=rows // CHUNK, slot=slot, hpb=hpb),
        grid=(batch, hblocks, nblk),
        in_specs=[pl.BlockSpec((n_slots, hpb * A_KEY), lambda b, h, c: (0, h)),
                  lane(0), lane(1), lane(2), lane(3)],
        out_specs=pl.BlockSpec((rows, hpb * A_KEY), lambda b, h, c: (b * nblk + c, h)),
        out_shape=jax.ShapeDtypeStruct((batch * seq, width), BF16),
        scratch_shapes=[pltpu.VMEM((hpb, A_KEY, A_KEY), F32)],
        compiler_params=_params(("parallel", "parallel", "arbitrary")),
        name="hgrn2",
    )(lb_param, z, z, z, z)


def _s5_tables(a_re, a_im, log_dt, b_re, b_im, c_re, c_im):
    hi = lax.Precision.HIGHEST
    L = S5_STEP
    groups, states = a_re.shape
    tiles = groups * B_GROUP // LANES
    gpt = LANES // B_GROUP
    dt = jnp.exp(log_dt.astype(F32))[:, None]
    lam_re = jnp.minimum(a_re.astype(F32), S5_DT_FLOOR)
    lam_im = a_im.astype(F32)
    mag = jnp.exp(lam_re * dt)
    ang = lam_im * dt
    ab_re = mag * jnp.cos(ang)
    ab_im = mag * jnp.sin(ang)
    nr = ab_re - 1.0
    ni = ab_im
    den = lam_re * lam_re + lam_im * lam_im
    z_re = (nr * lam_re + ni * lam_im) / den
    z_im = (ni * lam_re - nr * lam_im) / den
    br = b_re.astype(F32)
    bi = b_im.astype(F32)
    bb_re = z_re[..., None] * br - z_im[..., None] * bi
    bb_im = z_re[..., None] * bi + z_im[..., None] * br
    pw_re, pw_im = [jnp.ones_like(ab_re)], [jnp.zeros_like(ab_re)]
    for _ in range(L):
        pr, pi = pw_re[-1], pw_im[-1]
        pw_re.append(pr * ab_re - pi * ab_im)
        pw_im.append(pr * ab_im + pi * ab_re)
    pw_re = jnp.stack(pw_re)
    pw_im = jnp.stack(pw_im)
    cr = c_re.astype(F32)[None]
    ci = c_im.astype(F32)[None]
    cp_re = cr * pw_re[:, :, None, :] - ci * pw_im[:, :, None, :]
    cp_im = cr * pw_im[:, :, None, :] + ci * pw_re[:, :, None, :]
    lag = (jnp.einsum("lghp,gpk->lghk", cp_re[:L], bb_re, precision=hi)
           - jnp.einsum("lghp,gpk->lghk", cp_im[:L], bb_im, precision=hi))
    eye = jnp.eye(gpt, dtype=F32)
    lag = lag.reshape(L, tiles, gpt, B_GROUP, B_GROUP)
    kcat = jnp.einsum("ljghk,gm->jgklmh", lag, eye).reshape(tiles, LANES, L * LANES)
    rev = slice(L - 1, None, -1)
    wb_re = pw_re[rev][..., None] * bb_re[None] - pw_im[rev][..., None] * bb_im[None]
    wb_im = pw_re[rev][..., None] * bb_im[None] + pw_im[rev][..., None] * bb_re[None]
    wb = jnp.stack([wb_re, wb_im], axis=1)
    wb = wb.reshape(L, 2, tiles, gpt, states, B_GROUP)
    wst = jnp.einsum("srjgpk,gm->jsgkrmp", wb, eye).reshape(
        tiles, L, LANES, 2 * gpt * states)
    mo = jnp.stack([cp_re[1:], -cp_im[1:]], axis=1)
    mo = mo.reshape(L, 2, tiles, gpt, B_GROUP, states)
    mst = jnp.einsum("trjghp,gm->jrgptmh", mo, eye).reshape(
        tiles, 2 * gpt * states, L * LANES)
    al = jnp.stack([pw_re[L].reshape(tiles, gpt * states),
                    pw_im[L].reshape(tiles, gpt * states)], axis=1)
    return kcat.astype(BF16), wst.astype(BF16), mst.astype(BF16), al


def _s5_body(u_ref, kc_ref, ws_ref, ms_ref, al_ref, d_ref, o_ref, sl_ref, x_ref, y_ref, *, nc):
    L = S5_STEP
    half = sl_ref.shape[1] // 2
    step_rows = lambda s: u_ref[pl.ds(s, nc, stride=L), :]
    for s in range(L):
        part = jnp.dot(step_rows(s).astype(BF16), ws_ref[s], preferred_element_type=F32)
        if s == 0:
            sl_ref[...] = part
        else:
            sl_ref[...] += part
    alr = al_ref[0:1, :]
    ali = al_ref[1:2, :]

    def step(c, carry):
        xr, xi = carry
        row = pl.ds(c, 1)
        x_ref[row, 0:half] = xr
        x_ref[row, half:] = xi
        sr = sl_ref[row, 0:half]
        si = sl_ref[row, half:]
        return alr * xr - ali * xi + sr, alr * xi + ali * xr + si

    zero = jnp.zeros((1, half), F32)
    lax.fori_loop(0, nc, step, (zero, zero))
    y_ref[...] = jnp.dot(x_ref[...].astype(BF16), ms_ref[...], preferred_element_type=F32)
    for s in range(L):
        y_ref[:, s * LANES:] += jnp.dot(step_rows(s).astype(BF16),
                                        kc_ref[:, :(L - s) * LANES],
                                        preferred_element_type=F32)
    for t in range(L):
        y = y_ref[:, t * LANES:(t + 1) * LANES] + d_ref[...] * step_rows(t)
        o_ref[pl.ds(t, nc, stride=L), :] = _gelu(y)


def _s5(z, col_off, tables, d_skip, batch, seq, width):
    kcat, wst, mst, al = tables
    L = S5_STEP
    tiles = width // LANES
    nc = seq // L
    nstate = wst.shape[-1]
    off = col_off // LANES
    return pl.pallas_call(
        functools.partial(_s5_body, nc=nc),
        grid=(tiles, batch),
        in_specs=[
            pl.BlockSpec((seq, LANES), lambda j, b: (b, off + j)),
            pl.BlockSpec((None, LANES, L * LANES), lambda j, b: (j, 0, 0)),
            pl.BlockSpec((None, L, LANES, nstate), lambda j, b: (j, 0, 0, 0)),
            pl.BlockSpec((None, nstate, L * LANES), lambda j, b: (j, 0, 0)),
            pl.BlockSpec((None, 2, nstate // 2), lambda j, b: (j, 0, 0)),
            pl.BlockSpec((None, 1, LANES), lambda j, b: (j, 0, 0)),
        ],
        out_specs=pl.BlockSpec((seq, LANES), lambda j, b: (b, j)),
        out_shape=jax.ShapeDtypeStruct((batch * seq, width), F32),
        scratch_shapes=[pltpu.VMEM((nc, nstate), F32), pltpu.VMEM((nc, nstate), F32),
                        pltpu.VMEM((nc, L * LANES), F32)],
        compiler_params=_params(("parallel", "arbitrary")),
        name="s5",
    )(z, kcat, wst, mst, al, d_skip.astype(F32).reshape(tiles, 1, LANES))


def _ret_body(lg_ref, q_ref, k_ref, v_ref, g_ref, cos_ref, sin_ref, o_ref, r_ref, *, dk):
    @pl.when(pl.program_id(2) == 0)
    def _():
        r_ref[...] = jnp.zeros_like(r_ref)

    n = RET_BLOCK
    lg = lg_ref[pl.program_id(1)]
    cos = cos_ref[...]
    sin = sin_ref[...]
    hd = dk // 2

    def rot(t):
        t1, t2 = t[:, :hd], t[:, hd:]
        return jnp.concatenate([t1 * cos - t2 * sin, t2 * cos + t1 * sin], axis=1)

    q = rot(q_ref[...].astype(F32))
    k = rot(k_ref[...].astype(F32)) * (dk ** -0.5)
    ti = lax.broadcasted_iota(jnp.int32, (n, n), 0)
    si = lax.broadcasted_iota(jnp.int32, (n, n), 1)
    dist = jnp.abs(ti - si).astype(F32)
    decay = jnp.where(si // CHUNK <= ti // CHUNK, jnp.exp(dist * lg), 0.0)
    sc = lax.dot_general(q.astype(BF16), k.astype(BF16), _NT, preferred_element_type=F32) * decay
    v = v_ref[...]
    pos = lax.broadcasted_iota(jnp.int32, (n, 1), 0).astype(F32)
    qd = jnp.exp((pos + 1.0) * lg)
    kd = jnp.exp((n - 1.0 - pos) * lg)
    r = r_ref[...]
    o = (jnp.dot(sc.astype(BF16), v, preferred_element_type=F32)
         + jnp.dot((q * qd).astype(BF16), r.astype(BF16), preferred_element_type=F32))
    r_ref[...] = jnp.exp(n * lg) * r + lax.dot_general((k * kd).astype(BF16), v, _TN,
                                                       preferred_element_type=F32)
    o = o * lax.rsqrt(jnp.mean(o * o, axis=-1, keepdims=True) + EPS)
    g = g_ref[...].astype(F32)
    o_ref[...] = (o * g * _sigmoid(g)).astype(o_ref.dtype)


def _retention(z, batch, seq, d_qk, d_v):
    dk = d_qk // C_HEADS
    dv = d_v // C_HEADS
    n = RET_BLOCK
    nblk = seq // n
    pos = jnp.arange(seq, dtype=F32)
    inv_freq = ROPE_BASE ** (-jnp.arange(0, dk, 2, dtype=F32) / dk)
    ang = pos[:, None] * inv_freq[None, :]
    log_gamma = jnp.log(1.0 - 2.0 ** (-5.0 - jnp.arange(C_HEADS, dtype=F32)))
    row = lambda b, h, c, lg: b * nblk + c
    grid_spec = pltpu.PrefetchScalarGridSpec(
        num_scalar_prefetch=1,
        grid=(batch, C_HEADS, nblk),
        in_specs=[
            pl.BlockSpec((n, dk), lambda b, h, c, lg: (row(b, h, c, lg), h)),
            pl.BlockSpec((n, dk), lambda b, h, c, lg: (row(b, h, c, lg), C_HEADS + h)),
            pl.BlockSpec((n, dv), lambda b, h, c, lg: (row(b, h, c, lg), 2 * d_qk // dv + h)),
            pl.BlockSpec((n, dv), lambda b, h, c, lg: (row(b, h, c, lg),
                                                       (2 * d_qk + d_v) // dv + h)),
            pl.BlockSpec((n, dk // 2), lambda b, h, c, lg: (c, 0)),
            pl.BlockSpec((n, dk // 2), lambda b, h, c, lg: (c, 0)),
        ],
        out_specs=pl.BlockSpec((n, dv), lambda b, h, c, lg: (row(b, h, c, lg), h)),
        scratch_shapes=[pltpu.VMEM((dk, dv), F32)],
    )
    return pl.pallas_call(
        functools.partial(_ret_body, dk=dk),
        grid_spec=grid_spec,
        out_shape=jax.ShapeDtypeStruct((batch * seq, d_v), BF16),
        compiler_params=_params(("parallel", "parallel", "arbitrary")),
        name="retention",
    )(log_gamma, z, z, z, z, jnp.cos(ang), jnp.sin(ang))


def _fold_body(k_ref, w_ref, o_ref):
    o_ref[...] = lax.dot_general(k_ref[...].astype(BF16), w_ref[...].astype(BF16), _NT,
                                 preferred_element_type=F32).astype(o_ref.dtype)


def _peer_fold(sub_keys, w_q, *, td=1024):
    heads, two, nk, kh = sub_keys.shape
    d = w_q.shape[0]
    td = min(td, d)
    return pl.pallas_call(
        _fold_body,
        grid=(heads * two, d // td),
        in_specs=[pl.BlockSpec((nk, kh), lambda hp, j: (hp, 0)),
                  pl.BlockSpec((td, kh), lambda hp, j: (j, hp))],
        out_specs=pl.BlockSpec((nk, td), lambda hp, j: (hp, j)),
        out_shape=jax.ShapeDtypeStruct((heads * two * nk, d), BF16),
        compiler_params=_params(("parallel", "parallel")),
        name="peer_fold",
    )(sub_keys.reshape(heads * two * nk, kh), w_q)


def _top_k_rows(s, k):
    rows = lax.broadcasted_iota(jnp.int32, s.shape, 0).astype(F32)
    big = float(s.shape[0])
    rank = jnp.full(s.shape, float(k), F32)
    vals = []
    for i in range(k):
        m = jnp.max(s, axis=0, keepdims=True)
        first = jnp.min(jnp.where(s == m, rows, big), axis=0, keepdims=True)
        vals.append(m)
        hit = rows == first
        rank = jnp.where(hit, float(i), rank)
        s = jnp.where(hit, -jnp.inf, s)
    return jnp.concatenate(vals, axis=0), rank


def _peer_topk_body(sc_ref, e1_ref, e2_ref):
    nk = PEER_NKEYS
    k = PEER_TOPK
    s1 = sc_ref[0:nk, :]
    s2 = sc_ref[nk:2 * nk, :]
    t1, rank1 = _top_k_rows(s1, k)
    t2, rank2 = _top_k_rows(s2, k)
    sub = lax.broadcasted_iota(jnp.int32, (SUBLANES, 1), 0)
    blocks = [t1[0:1] + t2]
    for a in range(1, k):
        blocks.append(jnp.where(sub < k // (a + 1), t1[a:a + 1] + t2[0:SUBLANES], -jnp.inf))
    top, crank = _top_k_rows(jnp.concatenate(blocks, axis=0), k)
    chosen = (crank < float(k)).astype(F32)
    cnt_a = [jnp.sum(chosen[0:k], axis=0, keepdims=True)]
    for a in range(1, k):
        lo = k + (a - 1) * SUBLANES
        cnt_a.append(jnp.sum(chosen[lo:lo + SUBLANES], axis=0, keepdims=True))
    cnt = jnp.zeros(s1.shape, F32)
    for a in range(k):
        cnt = cnt + jnp.where(rank1 == float(a), cnt_a[a], 0.0)
    z = jnp.sum(jnp.exp(top - top[0:1]), axis=0, keepdims=True)
    e1_ref[0:nk, :] = jnp.exp(s1 - t1[0:1]) / z
    e1_ref[nk:2 * nk, :] = cnt
    e2_ref[0:nk, :] = jnp.exp(s2 - t2[0:1])
    e2_ref[nk:2 * nk, :] = rank2


def _peer_topk(sc_t, heads, *, tt=256):
    rows, t = sc_t.shape
    per = rows // heads
    tt = math.gcd(tt, t)
    spec = pl.BlockSpec((per, tt), lambda i, h: (h, i))
    return pl.pallas_call(
        _peer_topk_body,
        grid=(t // tt, heads),
        in_specs=[spec],
        out_specs=[spec, spec],
        out_shape=[jax.ShapeDtypeStruct((rows, t), F32)] * 2,
        compiler_params=_params(("parallel", "parallel")),
        name="peer_topk",
    )(sc_t)


def _peer_main_body(xn_ref, u_ref, v_ref, e1_ref, e2_ref, o_ref, p_ref, w_ref, *, heads):
    j = pl.program_id(1)

    @pl.when(j == 0)
    def _():
        o_ref[...] = jnp.zeros_like(o_ref)

    nk = PEER_NKEYS
    te, tt = p_ref.shape
    n_e1 = te // nk
    sub = lax.broadcasted_iota(jnp.int32, (SUBLANES, 1), 0)

    def key_row(row):
        start = pl.multiple_of((row // SUBLANES) * SUBLANES, SUBLANES)
        grp = e1_ref[pl.ds(start, SUBLANES), :]
        return jnp.sum(jnp.where(sub == row % SUBLANES, grp, 0.0), axis=0, keepdims=True)

    for r in range(n_e1):
        e1 = j * n_e1 + r
        rows = slice(r * nk, (r + 1) * nk)
        a1 = [key_row(h * 2 * nk + e1) for h in range(heads)]
        cnt = [key_row(h * 2 * nk + nk + e1) for h in range(heads)]
        for c in range(tt // LANES):
            lanes = slice(c * LANES, (c + 1) * LANES)
            w = jnp.zeros((nk, LANES), F32)
            for h in range(heads):
                base = h * 2 * nk
                a2 = e2_ref[base:base + nk, lanes]
                rank2 = e2_ref[base + nk:base + 2 * nk, lanes]
                w = w + jnp.where(rank2 < cnt[h][:, lanes], a2 * a1[h][:, lanes], 0.0)
            w_ref[rows, lanes] = w
    act = _gelu(lax.dot_general(u_ref[...], xn_ref[...], _NT, preferred_element_type=F32))
    p_ref[...] = (act * w_ref[...]).astype(p_ref.dtype)
    o_ref[...] += lax.dot_general(p_ref[...], v_ref[...], _TN, preferred_element_type=F32)


def _peer_main(xn, u16, v16, e1_t, e2_t, heads, *, tt=512, te=512):
    t, d = xn.shape
    n_exp = u16.shape[0]
    tt = min(tt, t)
    rows = e1_t.shape[0]
    once = dict(pipeline_mode=pl.Buffered(1))
    return pl.pallas_call(
        functools.partial(_peer_main_body, heads=heads),
        grid=(t // tt, n_exp // te),
        in_specs=[
            pl.BlockSpec((tt, d), lambda i, j: (i, 0), **once),
            pl.BlockSpec((te, d), lambda i, j: (j, 0)),
            pl.BlockSpec((te, d), lambda i, j: (j, 0)),
            pl.BlockSpec((rows, tt), lambda i, j: (0, i), **once),
            pl.BlockSpec((rows, tt), lambda i, j: (0, i), **once),
        ],
        out_specs=pl.BlockSpec((tt, d), lambda i, j: (i, 0)),
        out_shape=jax.ShapeDtypeStruct((t, d), F32),
        scratch_shapes=[pltpu.VMEM((te, tt), BF16), pltpu.VMEM((te, tt), F32)],
        compiler_params=_params(("parallel", "arbitrary")),
        name="peer_main",
    )(xn, u16, v16, e1_t, e2_t)


def _peer(xn, w_q, sub_keys, u_tab, v_tab):
    heads = sub_keys.shape[0]
    fold = _peer_fold(sub_keys, w_q)
    sc_t = _matmul(fold, xn, trans_b=True, name="peer_scores")
    e1_t, e2_t = _peer_topk(sc_t, heads)
    return _peer_main(xn, u_tab.astype(BF16), v_tab.astype(BF16), e1_t, e2_t, heads)


def kernel(x, mix_norm_g, ab_w_in, a_lb_param, b_a_re, b_a_im, b_log_dt, b_b_re, b_b_im, b_c_re, b_c_im, b_d, b_w_glu, ab_w_out, c_w_in, c_w_out, ffn_norm_g, peer_w_q, peer_sub_keys, peer_u, peer_v, final_norm_g):
    batch, seq, d = x.shape
    depth = mix_norm_g.shape[0]
    xs = x.reshape(batch * seq, d)
    delta = None
    for layer in range(depth):
        j = layer // 2
        if delta is None:
            h = _rmsnorm(xs, mix_norm_g[layer])
        else:
            xs, h = _rmsnorm(xs, mix_norm_g[layer], delta, emit_x=True)
        if layer % 2 == 0:
            a_width = a_lb_param.shape[1]
            b_width = b_w_glu.shape[1]
            z = _matmul(h, ab_w_in[j].astype(BF16), name="ab_in")
            o_a = _hgrn2(z, a_lb_param.astype(F32), j, batch, seq, a_width)
            tables = _s5_tables(b_a_re[j], b_a_im[j], b_log_dt[j], b_b_re[j], b_b_im[j],
                                b_c_re[j], b_c_im[j])
            yg = _s5(z, 4 * a_width, tables, b_d[j], batch, seq, b_width)
            o_b = _matmul(yg, b_w_glu[j].astype(BF16), extra=yg, epilogue="glu",
                          out_dtype=BF16, name="s5_glu")
            o = jnp.concatenate([o_a, o_b], axis=-1)
            xs = _matmul(o, ab_w_out[j].astype(BF16), extra=xs, epilogue="residual",
                         tm=512, name="ab_out")
        else:
            d_v = c_w_out.shape[1]
            d_qk = (c_w_in.shape[2] - 2 * d_v) // 2
            z = _matmul(h, c_w_in[j].astype(BF16), out_dtype=BF16, name="ret_in")
            o = _retention(z, batch, seq, d_qk, d_v)
            xs = _matmul(o, c_w_out[j].astype(BF16), extra=xs, epilogue="residual",
                         tm=512, tn=512, tk=8192, name="ret_out")
        hn = _rmsnorm(xs, ffn_norm_g[layer])
        delta = _peer(hn, peer_w_q[layer], peer_sub_keys[layer], peer_u[layer], peer_v[layer])
    out = _rmsnorm(xs, final_norm_g, delta, out_dtype=F32)
    return out.reshape(batch, seq, d)
```

```python
import functools
import math

import jax
import jax.numpy as jnp
from jax import lax
from jax.experimental import pallas as pl
from jax.experimental.pallas import tpu as pltpu

EPS = 1e-6
CHUNK = 64
A_KEY = 128
B_GROUP = 16
B_STATE = 64
S5_DT_FLOOR = -1e-4
S5_STEP = 16
C_HEADS = 16
ROPE_BASE = 10000.0
RET_BLOCK = 4 * CHUNK
RET_HEADS_PER_STEP = 2
PEER_TOPK = 16
PEER_NKEYS = 128

LANES = 128
SUBLANES = 8
V7X_VMEM_LIMIT = 56 * 1024 * 1024

F32 = jnp.float32
BF16 = jnp.bfloat16

_NN = (((1,), (0,)), ((), ()))
_NT = (((1,), (1,)), ((), ()))
_TN = (((0,), (0,)), ((), ()))


def _params(sem):
    return pltpu.CompilerParams(dimension_semantics=sem, vmem_limit_bytes=V7X_VMEM_LIMIT)


def _sigmoid(x):
    return 1.0 / (1.0 + jnp.exp(-x))


def _gelu(x):
    return 0.5 * x * (1.0 + lax.erf(x * math.sqrt(0.5)))


def _norm_body(*refs, has_delta, emit_x):
    x_ref = refs[0]
    d_ref = refs[1] if has_delta else None
    g_ref = refs[1 + has_delta]
    outs = refs[2 + has_delta:]
    x = x_ref[...]
    if has_delta:
        x = x + d_ref[...]
    if emit_x:
        outs[0][...] = x
    h_ref = outs[-1]
    ms = jnp.mean(x * x, axis=-1, keepdims=True)
    h_ref[...] = (x * lax.rsqrt(ms + EPS) * g_ref[...]).astype(h_ref.dtype)


def _rmsnorm(x, g, delta=None, *, emit_x=False, out_dtype=BF16, tm=256):
    m, d = x.shape
    has_delta = delta is not None
    row = pl.BlockSpec((tm, d), lambda i: (i, 0))
    in_specs = [row] + ([row] if has_delta else []) + [pl.BlockSpec((1, d), lambda i: (0, 0))]
    args = [x] + ([delta] if has_delta else []) + [g.reshape(1, d)]
    out_shape = [jax.ShapeDtypeStruct((m, d), out_dtype)]
    out_specs = [row]
    if emit_x:
        out_shape = [jax.ShapeDtypeStruct((m, d), F32)] + out_shape
        out_specs = [row] + out_specs
    res = pl.pallas_call(
        functools.partial(_norm_body, has_delta=has_delta, emit_x=emit_x),
        grid=(m // tm,),
        in_specs=in_specs,
        out_specs=out_specs,
        out_shape=out_shape,
        compiler_params=_params(("parallel",)),
        name="rmsnorm",
    )(*args)
    return res if emit_x else res[0]


def _mm_body(*refs, nk, trans_b, epilogue, aliased):
    a_ref, b_ref = refs[0], refs[1]
    e_ref = refs[2] if epilogue else None
    n_in = 2 + bool(epilogue) + aliased
    o_ref = refs[n_in]
    acc_ref = refs[n_in + 1] if nk > 1 else None
    part = lax.dot_general(a_ref[...].astype(b_ref.dtype), b_ref[...], _NT if trans_b else _NN,
                           preferred_element_type=F32)

    def finish(acc):
        if epilogue == "residual":
            acc = acc + e_ref[...]
        elif epilogue == "glu":
            acc = e_ref[...].astype(F32) * _sigmoid(acc)
        o_ref[...] = acc.astype(o_ref.dtype)

    if nk == 1:
        finish(part)
        return
    k = pl.program_id(2)

    @pl.when(k == 0)
    def _():
        acc_ref[...] = part

    @pl.when(k > 0)
    def _():
        acc_ref[...] += part

    @pl.when(k == nk - 1)
    def _():
        finish(acc_ref[...])


def _matmul(a, b, *, trans_b=False, extra=None, epilogue=None, out_dtype=F32, into=None,
            tm=1024, tn=1024, tk=4096, name="matmul"):
    m, kdim = a.shape
    n = b.shape[0] if trans_b else b.shape[1]
    tm, tn, tk = math.gcd(tm, m), math.gcd(tn, n), math.gcd(tk, kdim)
    if into is not None:
        tn = math.gcd(tn, into[1])
    nk = kdim // tk
    a_spec = pl.BlockSpec((tm, tk), lambda i, j, k: (i, k))
    if trans_b:
        b_spec = pl.BlockSpec((tn, tk), lambda i, j, k: (j, k))
    else:
        b_spec = pl.BlockSpec((tk, tn), lambda i, j, k: (k, j))
    e_spec = pl.BlockSpec((tm, tn), lambda i, j, k: (i, j))
    in_specs = [a_spec, b_spec] + ([e_spec] if epilogue else [])
    args = [a, b] + ([extra] if epilogue else [])
    if into is None:
        o_spec, out_shape, aliases = e_spec, jax.ShapeDtypeStruct((m, n), out_dtype), {}
    else:
        dest, col = into
        joff = col // tn
        o_spec = pl.BlockSpec((tm, tn), lambda i, j, k: (i, j + joff))
        out_shape = jax.ShapeDtypeStruct(dest.shape, dest.dtype)
        aliases = {len(args): 0}
        in_specs = in_specs + [pl.BlockSpec(memory_space=pl.ANY)]
        args = args + [dest]
    scratch = [pltpu.VMEM((tm, tn), F32)] if nk > 1 else []
    return pl.pallas_call(
        functools.partial(_mm_body, nk=nk, trans_b=trans_b, epilogue=epilogue,
                          aliased=into is not None),
        grid=(m // tm, n // tn, nk),
        in_specs=in_specs,
        out_specs=o_spec,
        out_shape=out_shape,
        input_output_aliases=aliases,
        scratch_shapes=scratch,
        compiler_params=_params(("parallel", "parallel", "arbitrary")),
        name=name,
    )(*args)


HG_SUB = 16


def _hgrn2_body(lbp_ref, q_ref, f_ref, i_ref, g_ref, o_ref, st_ref, *, n_chunks, slot, hpb):
    @pl.when(pl.program_id(2) == 0)
    def _():
        st_ref[...] = jnp.zeros_like(st_ref)

    p = lbp_ref[...]
    ex = jnp.exp(p - jnp.max(p, axis=0, keepdims=True))
    sm = ex / jnp.sum(ex, axis=0, keepdims=True)
    lb = jnp.sum(sm[: slot + 1], axis=0, keepdims=True)

    ri = lax.broadcasted_iota(jnp.int32, (CHUNK, CHUNK), 0)
    ci = lax.broadcasted_iota(jnp.int32, (CHUNK, CHUNK), 1)
    tri = (ci <= ri).astype(F32)
    sub_row = lax.broadcasted_iota(jnp.int32, (HG_SUB, 1), 0)
    n_sub = CHUNK // HG_SUB
    heads_cols = [slice(hh * A_KEY, (hh + 1) * A_KEY) for hh in range(hpb)]

    def per_head(fn):
        return jnp.concatenate([fn(hh, cols) for hh, cols in enumerate(heads_cols)], axis=1)

    def head_sum(x):
        return per_head(lambda hh, cols: jnp.broadcast_to(
            jnp.sum(x[:, cols], axis=-1, keepdims=True), (x.shape[0], A_KEY)))

    def chunk(ci_, carry):
        rows = pl.ds(pl.multiple_of(ci_ * CHUNK, CHUNK), CHUNK)
        f = lb + (1.0 - lb) * _sigmoid(f_ref[rows, :])
        qr = q_ref[rows, :]
        q = qr * _sigmoid(qr)
        k = 1.0 - f
        v = i_ref[rows, :]
        cum = jnp.dot(tri, jnp.log(f), precision=lax.Precision.HIGHEST,
                      preferred_element_type=F32)
        v16 = v.astype(BF16)
        parts = []
        for sb in range(n_sub):
            lo = sb * HG_SUB
            cs = cum[lo:lo + HG_SUB]
            qs = q[lo:lo + HG_SUB]
            ks = k[lo:lo + HG_SUB]
            vs = v[lo:lo + HG_SUB]
            acc = jnp.zeros(cs.shape, F32)
            for s in range(HG_SUB):
                diff = jnp.where(sub_row >= s, cs - cs[s:s + 1], -jnp.inf)
                acc = acc + head_sum(qs * jnp.exp(diff) * ks[s:s + 1]) * vs[s:s + 1]
            if sb > 0:
                ref = cum[lo - 1:lo]
                qx = (qs * jnp.exp(cs - ref)).astype(BF16)
                kx = (k[:lo] * jnp.exp(ref - cum[:lo])).astype(BF16)

                def off_diag(hh, cols):
                    att = lax.dot_general(qx[:, cols], kx[:, cols], _NT,
                                          preferred_element_type=F32)
                    return jnp.dot(att.astype(BF16), v16[:lo, cols], preferred_element_type=F32)

                acc = acc + per_head(off_diag)
            parts.append(acc)
        o = jnp.concatenate(parts, axis=0)
        last = cum[CHUNK - 1:CHUNK]
        qd = (q * jnp.exp(cum)).astype(BF16)
        kd = (k * jnp.exp(last - cum)).astype(BF16)
        decay = jnp.exp(last)
        states = [st_ref[hh] for hh in range(hpb)]
        o = o + per_head(lambda hh, cols: lax.dot_general(
            qd[:, cols], states[hh].astype(BF16), _NT, preferred_element_type=F32))
        for hh, cols in enumerate(heads_cols):
            st_ref[hh] = states[hh] * decay[:, cols] + lax.dot_general(
                v16[:, cols], kd[:, cols], _TN, preferred_element_type=F32)
        o = o * lax.rsqrt(head_sum(o * o) * (1.0 / A_KEY) + EPS)
        gr = g_ref[rows, :]
        o_ref[rows, :] = (o * gr * _sigmoid(gr)).astype(o_ref.dtype)
        return carry

    lax.fori_loop(0, n_chunks, chunk, 0)


def _hgrn2(z, lb_param, slot, batch, seq, width, out_width, *, rows=512, heads_per_block=8):
    heads = width // A_KEY
    hpb = math.gcd(heads_per_block, heads)
    hblocks = heads // hpb
    rows = math.gcd(rows, seq)
    nblk = seq // rows
    lane = lambda off: pl.BlockSpec(
        (rows, hpb * A_KEY), lambda b, h, c, off=off: (b * nblk + c, off * hblocks + h))
    n_slots = lb_param.shape[0]
    return pl.pallas_call(
        functools.partial(_hgrn2_body, n_chunks=rows // CHUNK, slot=slot, hpb=hpb),
        grid=(batch, hblocks, nblk),
        in_specs=[pl.BlockSpec((n_slots, hpb * A_KEY), lambda b, h, c: (0, h)),
                  lane(0), lane(1), lane(2), lane(3)],
        out_specs=pl.BlockSpec((rows, hpb * A_KEY), lambda b, h, c: (b * nblk + c, h)),
        out_shape=jax.ShapeDtypeStruct((batch * seq, out_width), BF16),
        scratch_shapes=[pltpu.VMEM((hpb, A_KEY, A_KEY), F32)],
        compiler_params=_params(("parallel", "parallel", "arbitrary")),
        name="hgrn2",
    )(lb_param, z, z, z, z)


def _s5_tables(a_re, a_im, log_dt, b_re, b_im, c_re, c_im):
    hi = lax.Precision.HIGHEST
    L = S5_STEP
    groups, states = a_re.shape
    tiles = groups * B_GROUP // LANES
    gpt = LANES // B_GROUP
    dt = jnp.exp(log_dt.astype(F32))[:, None]
    lam_re = jnp.minimum(a_re.astype(F32), S5_DT_FLOOR)
    lam_im = a_im.astype(F32)
    mag = jnp.exp(lam_re * dt)
    ang = lam_im * dt
    ab_re = mag * jnp.cos(ang)
    ab_im = mag * jnp.sin(ang)
    nr = ab_re - 1.0
    ni = ab_im
    den = lam_re * lam_re + lam_im * lam_im
    z_re = (nr * lam_re + ni * lam_im) / den
    z_im = (ni * lam_re - nr * lam_im) / den
    br = b_re.astype(F32)
    bi = b_im.astype(F32)
    bb_re = z_re[..., None] * br - z_im[..., None] * bi
    bb_im = z_re[..., None] * bi + z_im[..., None] * br
    pw_re, pw_im = [jnp.ones_like(ab_re)], [jnp.zeros_like(ab_re)]
    for _ in range(L):
        pr, pi = pw_re[-1], pw_im[-1]
        pw_re.append(pr * ab_re - pi * ab_im)
        pw_im.append(pr * ab_im + pi * ab_re)
    pw_re = jnp.stack(pw_re)
    pw_im = jnp.stack(pw_im)
    cr = c_re.astype(F32)[None]
    ci = c_im.astype(F32)[None]
    cp_re = cr * pw_re[:, :, None, :] - ci * pw_im[:, :, None, :]
    cp_im = cr * pw_im[:, :, None, :] + ci * pw_re[:, :, None, :]
    lag = (jnp.einsum("lghp,gpk->lghk", cp_re[:L], bb_re, precision=hi)
           - jnp.einsum("lghp,gpk->lghk", cp_im[:L], bb_im, precision=hi))
    eye = jnp.eye(gpt, dtype=BF16)
    lag = lag.reshape(L, tiles, gpt, B_GROUP, B_GROUP).astype(BF16)
    lag = lag.transpose(1, 2, 4, 0, 3)
    kcat = (lag[:, :, :, :, None, :] * eye[None, :, None, None, :, None]).reshape(
        tiles, LANES, L * LANES)
    rev = slice(L - 1, None, -1)
    wb_re = pw_re[rev][..., None] * bb_re[None] - pw_im[rev][..., None] * bb_im[None]
    wb_im = pw_re[rev][..., None] * bb_im[None] + pw_im[rev][..., None] * bb_re[None]
    wb = jnp.stack([wb_re, wb_im], axis=1).astype(BF16)
    wb = wb.reshape(L, 2, tiles, gpt, states, B_GROUP).transpose(2, 0, 3, 5, 1, 4)
    wst = (wb[:, :, :, :, :, None, :] * eye[None, None, :, None, None, :, None]).reshape(
        tiles, L, LANES, 2 * gpt * states)
    mo = jnp.stack([cp_re[1:], -cp_im[1:]], axis=1).astype(BF16)
    mo = mo.reshape(L, 2, tiles, gpt, B_GROUP, states).transpose(2, 1, 3, 5, 0, 4)
    mst = (mo[:, :, :, :, :, None, :] * eye[None, None, :, None, None, :, None]).reshape(
        tiles, 2 * gpt * states, L * LANES)
    al = jnp.stack([pw_re[L].reshape(tiles, gpt * states),
                    pw_im[L].reshape(tiles, gpt * states)], axis=1)
    return kcat, wst, mst, al


def _s5_body(u_ref, kc_ref, ws_ref, ms_ref, al_ref, d_ref, o_ref, sl_ref, x_ref, y_ref, *, nc):
    L = S5_STEP
    half = sl_ref.shape[1] // 2
    step_rows = lambda s: u_ref[pl.ds(s, nc, stride=L), :]
    for s in range(L):
        part = jnp.dot(step_rows(s).astype(BF16), ws_ref[s], preferred_element_type=F32)
        if s == 0:
            sl_ref[...] = part
        else:
            sl_ref[...] += part
    alr = al_ref[0:1, :]
    ali = al_ref[1:2, :]

    def step(c, carry):
        xr, xi = carry
        row = pl.ds(c, 1)
        x_ref[row, 0:half] = xr
        x_ref[row, half:] = xi
        sr = sl_ref[row, 0:half]
        si = sl_ref[row, half:]
        return alr * xr - ali * xi + sr, alr * xi + ali * xr + si

    zero = jnp.zeros((1, half), F32)
    lax.fori_loop(0, nc, step, (zero, zero))
    y_ref[...] = jnp.dot(x_ref[...].astype(BF16), ms_ref[...], preferred_element_type=F32)
    for s in range(L):
        y_ref[:, s * LANES:] += jnp.dot(step_rows(s).astype(BF16),
                                        kc_ref[:, :(L - s) * LANES],
                                        preferred_element_type=F32)
    for t in range(L):
        y = y_ref[:, t * LANES:(t + 1) * LANES] + d_ref[...] * step_rows(t)
        o_ref[pl.ds(t, nc, stride=L), :] = _gelu(y)


def _s5(z, col_off, tables, d_skip, batch, seq, width):
    kcat, wst, mst, al = tables
    L = S5_STEP
    tiles = width // LANES
    nc = seq // L
    nstate = wst.shape[-1]
    off = col_off // LANES
    return pl.pallas_call(
        functools.partial(_s5_body, nc=nc),
        grid=(tiles, batch),
        in_specs=[
            pl.BlockSpec((seq, LANES), lambda j, b: (b, off + j)),
            pl.BlockSpec((None, LANES, L * LANES), lambda j, b: (j, 0, 0)),
            pl.BlockSpec((None, L, LANES, nstate), lambda j, b: (j, 0, 0, 0)),
            pl.BlockSpec((None, nstate, L * LANES), lambda j, b: (j, 0, 0)),
            pl.BlockSpec((None, 2, nstate // 2), lambda j, b: (j, 0, 0)),
            pl.BlockSpec((None, 1, LANES), lambda j, b: (j, 0, 0)),
        ],
        out_specs=pl.BlockSpec((seq, LANES), lambda j, b: (b, j)),
        out_shape=jax.ShapeDtypeStruct((batch * seq, width), F32),
        scratch_shapes=[pltpu.VMEM((nc, nstate), F32), pltpu.VMEM((nc, nstate), F32),
                        pltpu.VMEM((nc, L * LANES), F32)],
        compiler_params=_params(("parallel", "arbitrary")),
        name="s5",
    )(z, kcat, wst, mst, al, d_skip.astype(F32).reshape(tiles, 1, LANES))


def _ret_body(lg_ref, q_ref, k_ref, v_ref, g_ref, cos_ref, sin_ref, o_ref, r_ref, dec_ref,
              *, dk, dv, hp):
    n = RET_BLOCK
    head0 = pl.program_id(1) * hp

    @pl.when(pl.program_id(2) == 0)
    def _():
        r_ref[...] = jnp.zeros_like(r_ref)
        ti = lax.broadcasted_iota(jnp.int32, (n, n), 0)
        si = lax.broadcasted_iota(jnp.int32, (n, n), 1)
        dist = jnp.abs(ti - si).astype(F32)
        for i in range(hp):
            dec_ref[i] = jnp.where(si // CHUNK <= ti // CHUNK,
                                   jnp.exp(dist * lg_ref[head0 + i]), 0.0)

    cos = cos_ref[...]
    sin = sin_ref[...]
    hd = dk // 2
    pos = lax.broadcasted_iota(jnp.int32, (n, 1), 0).astype(F32)

    def rot(t):
        t1, t2 = t[:, :hd], t[:, hd:]
        return jnp.concatenate([t1 * cos - t2 * sin, t2 * cos + t1 * sin], axis=1)

    for i in range(hp):
        lg = lg_ref[head0 + i]
        qk_cols = slice(i * dk, (i + 1) * dk)
        v_cols = slice(i * dv, (i + 1) * dv)
        q = rot(q_ref[:, qk_cols].astype(F32))
        k = rot(k_ref[:, qk_cols].astype(F32)) * (dk ** -0.5)
        sc = lax.dot_general(q.astype(BF16), k.astype(BF16), _NT,
                             preferred_element_type=F32) * dec_ref[i]
        v = v_ref[:, v_cols]
        qd = jnp.exp((pos + 1.0) * lg)
        kd = jnp.exp((n - 1.0 - pos) * lg)
        r = r_ref[i]
        o = (jnp.dot(sc.astype(BF16), v, preferred_element_type=F32)
             + jnp.dot((q * qd).astype(BF16), r.astype(BF16), preferred_element_type=F32))
        r_ref[i] = jnp.exp(n * lg) * r + lax.dot_general((k * kd).astype(BF16), v, _TN,
                                                         preferred_element_type=F32)
        o = o * lax.rsqrt(jnp.mean(o * o, axis=-1, keepdims=True) + EPS)
        g = g_ref[:, v_cols].astype(F32)
        o_ref[:, v_cols] = (o * g * _sigmoid(g)).astype(o_ref.dtype)


def _retention(z, batch, seq, d_qk, d_v):
    dk = d_qk // C_HEADS
    dv = d_v // C_HEADS
    n = RET_BLOCK
    nblk = seq // n
    pos = jnp.arange(seq, dtype=F32)
    inv_freq = ROPE_BASE ** (-jnp.arange(0, dk, 2, dtype=F32) / dk)
    ang = pos[:, None] * inv_freq[None, :]
    log_gamma = jnp.log(1.0 - 2.0 ** (-5.0 - jnp.arange(C_HEADS, dtype=F32)))
    hp = math.gcd(RET_HEADS_PER_STEP, C_HEADS)
    hblocks = C_HEADS // hp
    wk, wv = hp * dk, hp * dv
    row = lambda b, h, c, lg: b * nblk + c
    grid_spec = pltpu.PrefetchScalarGridSpec(
        num_scalar_prefetch=1,
        grid=(batch, hblocks, nblk),
        in_specs=[
            pl.BlockSpec((n, wk), lambda b, h, c, lg: (row(b, h, c, lg), h)),
            pl.BlockSpec((n, wk), lambda b, h, c, lg: (row(b, h, c, lg), hblocks + h)),
            pl.BlockSpec((n, wv), lambda b, h, c, lg: (row(b, h, c, lg), 2 * d_qk // wv + h)),
            pl.BlockSpec((n, wv), lambda b, h, c, lg: (row(b, h, c, lg),
                                                       (2 * d_qk + d_v) // wv + h)),
            pl.BlockSpec((n, dk // 2), lambda b, h, c, lg: (c, 0)),
            pl.BlockSpec((n, dk // 2), lambda b, h, c, lg: (c, 0)),
        ],
        out_specs=pl.BlockSpec((n, wv), lambda b, h, c, lg: (row(b, h, c, lg), h)),
        scratch_shapes=[pltpu.VMEM((hp, dk, dv), F32), pltpu.VMEM((hp, n, n), F32)],
    )
    return pl.pallas_call(
        functools.partial(_ret_body, dk=dk, dv=dv, hp=hp),
        grid_spec=grid_spec,
        out_shape=jax.ShapeDtypeStruct((batch * seq, d_v), BF16),
        compiler_params=_params(("parallel", "parallel", "arbitrary")),
        name="retention",
    )(log_gamma, z, z, z, z, jnp.cos(ang), jnp.sin(ang))


def _fold_body(k_ref, w_ref, o_ref):
    o_ref[...] = lax.dot_general(k_ref[...].astype(BF16), w_ref[...].astype(BF16), _NT,
                                 preferred_element_type=F32).astype(o_ref.dtype)


def _peer_fold(sub_keys, w_q, *, td=1024):
    heads, two, nk, kh = sub_keys.shape
    d = w_q.shape[0]
    td = min(td, d)
    return pl.pallas_call(
        _fold_body,
        grid=(heads * two, d // td),
        in_specs=[pl.BlockSpec((nk, kh), lambda hp, j: (hp, 0)),
                  pl.BlockSpec((td, kh), lambda hp, j: (j, hp))],
        out_specs=pl.BlockSpec((nk, td), lambda hp, j: (hp, j)),
        out_shape=jax.ShapeDtypeStruct((heads * two * nk, d), BF16),
        compiler_params=_params(("parallel", "parallel")),
        name="peer_fold",
    )(sub_keys.reshape(heads * two * nk, kh), w_q)


def _top_k_rows(s, k):
    rows = lax.broadcasted_iota(jnp.int32, s.shape, 0).astype(F32)
    big = float(s.shape[0])
    rank = jnp.full(s.shape, float(k), F32)
    vals = []
    for i in range(k):
        m = jnp.max(s, axis=0, keepdims=True)
        first = jnp.min(jnp.where(s == m, rows, big), axis=0, keepdims=True)
        vals.append(m)
        hit = rows == first
        rank = jnp.where(hit, float(i), rank)
        s = jnp.where(hit, -jnp.inf, s)
    return jnp.concatenate(vals, axis=0), rank


def _peer_topk_body(sc_ref, e1_ref, e2_ref):
    nk = PEER_NKEYS
    k = PEER_TOPK
    s1 = sc_ref[0:nk, :]
    s2 = sc_ref[nk:2 * nk, :]
    t1, rank1 = _top_k_rows(s1, k)
    t2, rank2 = _top_k_rows(s2, k)
    sub = lax.broadcasted_iota(jnp.int32, (SUBLANES, 1), 0)
    blocks = [t1[0:1] + t2]
    for a in range(1, k):
        blocks.append(jnp.where(sub < k // (a + 1), t1[a:a + 1] + t2[0:SUBLANES], -jnp.inf))
    top, crank = _top_k_rows(jnp.concatenate(blocks, axis=0), k)
    chosen = (crank < float(k)).astype(F32)
    cnt_a = [jnp.sum(chosen[0:k], axis=0, keepdims=True)]
    for a in range(1, k):
        lo = k + (a - 1) * SUBLANES
        cnt_a.append(jnp.sum(chosen[lo:lo + SUBLANES], axis=0, keepdims=True))
    cnt = jnp.zeros(s1.shape, F32)
    for a in range(k):
        cnt = cnt + jnp.where(rank1 == float(a), cnt_a[a], 0.0)
    z = jnp.sum(jnp.exp(top - top[0:1]), axis=0, keepdims=True)
    e1_ref[0:nk, :] = jnp.exp(s1 - t1[0:1]) / z
    e1_ref[nk:2 * nk, :] = cnt
    e2_ref[0:nk, :] = jnp.exp(s2 - t2[0:1])
    e2_ref[nk:2 * nk, :] = rank2


def _peer_topk(sc_t, heads, *, tt=256):
    rows, t = sc_t.shape
    per = rows // heads
    tt = math.gcd(tt, t)
    spec = pl.BlockSpec((per, tt), lambda i, h: (h, i))
    return pl.pallas_call(
        _peer_topk_body,
        grid=(t // tt, heads),
        in_specs=[spec],
        out_specs=[spec, spec],
        out_shape=[jax.ShapeDtypeStruct((rows, t), F32)] * 2,
        compiler_params=_params(("parallel", "parallel")),
        name="peer_topk",
    )(sc_t)


def _peer_main_body(xn_ref, u_ref, v_ref, e1_ref, e2_ref, o_ref, p_ref, w_ref, *, heads):
    j = pl.program_id(1)

    @pl.when(j == 0)
    def _():
        o_ref[...] = jnp.zeros_like(o_ref)

    nk = PEER_NKEYS
    te, tt = p_ref.shape
    n_e1 = te // nk
    sub = lax.broadcasted_iota(jnp.int32, (SUBLANES, 1), 0)

    def key_row(row):
        start = pl.multiple_of((row // SUBLANES) * SUBLANES, SUBLANES)
        grp = e1_ref[pl.ds(start, SUBLANES), :]
        return jnp.sum(jnp.where(sub == row % SUBLANES, grp, 0.0), axis=0, keepdims=True)

    for r in range(n_e1):
        e1 = j * n_e1 + r
        rows = slice(r * nk, (r + 1) * nk)
        a1 = [key_row(h * 2 * nk + e1) for h in range(heads)]
        cnt = [key_row(h * 2 * nk + nk + e1) for h in range(heads)]
        for c in range(tt // LANES):
            lanes = slice(c * LANES, (c + 1) * LANES)
            w = jnp.zeros((nk, LANES), F32)
            for h in range(heads):
                base = h * 2 * nk
                a2 = e2_ref[base:base + nk, lanes]
                rank2 = e2_ref[base + nk:base + 2 * nk, lanes]
                w = w + jnp.where(rank2 < cnt[h][:, lanes], a2 * a1[h][:, lanes], 0.0)
            w_ref[rows, lanes] = w
    act = _gelu(lax.dot_general(u_ref[...], xn_ref[...], _NT, preferred_element_type=F32))
    p_ref[...] = (act * w_ref[...]).astype(p_ref.dtype)
    o_ref[...] += lax.dot_general(p_ref[...], v_ref[...], _TN, preferred_element_type=F32)


def _peer_main(xn, u16, v16, e1_t, e2_t, heads, *, tt=512, te=512):
    t, d = xn.shape
    n_exp = u16.shape[0]
    tt = min(tt, t)
    rows = e1_t.shape[0]
    once = dict(pipeline_mode=pl.Buffered(1))
    return pl.pallas_call(
        functools.partial(_peer_main_body, heads=heads),
        grid=(t // tt, n_exp // te),
        in_specs=[
            pl.BlockSpec((tt, d), lambda i, j: (i, 0), **once),
            pl.BlockSpec((te, d), lambda i, j: (j, 0)),
            pl.BlockSpec((te, d), lambda i, j: (j, 0)),
            pl.BlockSpec((rows, tt), lambda i, j: (0, i), **once),
            pl.BlockSpec((rows, tt), lambda i, j: (0, i), **once),
        ],
        out_specs=pl.BlockSpec((tt, d), lambda i, j: (i, 0)),
        out_shape=jax.ShapeDtypeStruct((t, d), F32),
        scratch_shapes=[pltpu.VMEM((te, tt), BF16), pltpu.VMEM((te, tt), F32)],
        compiler_params=_params(("parallel", "arbitrary")),
        name="peer_main",
    )(xn, u16, v16, e1_t, e2_t)


def _peer(xn, w_q, sub_keys, u_tab, v_tab):
    heads = sub_keys.shape[0]
    fold = _peer_fold(sub_keys, w_q)
    sc_t = _matmul(fold, xn, trans_b=True, name="peer_scores")
    e1_t, e2_t = _peer_topk(sc_t, heads)
    return _peer_main(xn, u_tab.astype(BF16), v_tab.astype(BF16), e1_t, e2_t, heads)


def kernel(x, mix_norm_g, ab_w_in, a_lb_param, b_a_re, b_a_im, b_log_dt, b_b_re, b_b_im, b_c_re, b_c_im, b_d, b_w_glu, ab_w_out, c_w_in, c_w_out, ffn_norm_g, peer_w_q, peer_sub_keys, peer_u, peer_v, final_norm_g):
    batch, seq, d = x.shape
    depth = mix_norm_g.shape[0]
    xs = x.reshape(batch * seq, d)
    delta = None
    for layer in range(depth):
        j = layer // 2
        if delta is None:
            h = _rmsnorm(xs, mix_norm_g[layer])
        else:
            xs, h = _rmsnorm(xs, mix_norm_g[layer], delta, emit_x=True)
        if layer % 2 == 0:
            a_width = a_lb_param.shape[1]
            b_width = b_w_glu.shape[1]
            z = _matmul(h, ab_w_in[j].astype(BF16), name="ab_in")
            o = _hgrn2(z, a_lb_param.astype(F32), j, batch, seq, a_width, a_width + b_width)
            tables = _s5_tables(b_a_re[j], b_a_im[j], b_log_dt[j], b_b_re[j], b_b_im[j],
                                b_c_re[j], b_c_im[j])
            yg = _s5(z, 4 * a_width, tables, b_d[j], batch, seq, b_width)
            o = _matmul(yg, b_w_glu[j].astype(BF16), extra=yg, epilogue="glu",
                        into=(o, a_width), name="s5_glu")
            xs = _matmul(o, ab_w_out[j].astype(BF16), extra=xs, epilogue="residual",
                         tm=512, name="ab_out")
        else:
            d_v = c_w_out.shape[1]
            d_qk = (c_w_in.shape[2] - 2 * d_v) // 2
            z = _matmul(h, c_w_in[j].astype(BF16), out_dtype=BF16, name="ret_in")
            o = _retention(z, batch, seq, d_qk, d_v)
            xs = _matmul(o, c_w_out[j].astype(BF16), extra=xs, epilogue="residual",
                         tm=512, tn=512, tk=8192, name="ret_out")
        hn = _rmsnorm(xs, ffn_norm_g[layer])
        delta = _peer(hn, peer_w_q[layer], peer_sub_keys[layer], peer_u[layer], peer_v[layer])
    out = _rmsnorm(xs, final_norm_g, delta, out_dtype=F32)
    return out.reshape(batch, seq, d)
```

```python
import functools
import math

import jax
import jax.numpy as jnp
from jax import lax
from jax.experimental import pallas as pl
from jax.experimental.pallas import tpu as pltpu

EPS = 1e-6
CHUNK = 64
A_KEY = 128
B_GROUP = 16
B_STATE = 64
S5_DT_FLOOR = -1e-4
S5_STEP = 16
C_HEADS = 16
ROPE_BASE = 10000.0
RET_BLOCK = 4 * CHUNK
RET_HEADS_PER_STEP = 2
PEER_TOPK = 16
PEER_NKEYS = 128

LANES = 128
SUBLANES = 8
V7X_VMEM_LIMIT = 56 * 1024 * 1024

F32 = jnp.float32
BF16 = jnp.bfloat16

_NN = (((1,), (0,)), ((), ()))
_NT = (((1,), (1,)), ((), ()))
_TN = (((0,), (0,)), ((), ()))


def _params(sem):
    return pltpu.CompilerParams(dimension_semantics=sem, vmem_limit_bytes=V7X_VMEM_LIMIT)


def _sigmoid(x):
    return 1.0 / (1.0 + jnp.exp(-x))


def _gelu(x):
    return 0.5 * x * (1.0 + lax.erf(x * math.sqrt(0.5)))


def _norm_body(*refs, has_delta, emit_x):
    x_ref = refs[0]
    d_ref = refs[1] if has_delta else None
    g_ref = refs[1 + has_delta]
    outs = refs[2 + has_delta:]
    x = x_ref[...]
    if has_delta:
        x = x + d_ref[...]
    if emit_x:
        outs[0][...] = x
    h_ref = outs[-1]
    ms = jnp.mean(x * x, axis=-1, keepdims=True)
    h_ref[...] = (x * lax.rsqrt(ms + EPS) * g_ref[...]).astype(h_ref.dtype)


def _rmsnorm(x, g, delta=None, *, emit_x=False, out_dtype=BF16, tm=256):
    m, d = x.shape
    has_delta = delta is not None
    row = pl.BlockSpec((tm, d), lambda i: (i, 0))
    in_specs = [row] + ([row] if has_delta else []) + [pl.BlockSpec((1, d), lambda i: (0, 0))]
    args = [x] + ([delta] if has_delta else []) + [g.reshape(1, d)]
    out_shape = [jax.ShapeDtypeStruct((m, d), out_dtype)]
    out_specs = [row]
    if emit_x:
        out_shape = [jax.ShapeDtypeStruct((m, d), F32)] + out_shape
        out_specs = [row] + out_specs
    res = pl.pallas_call(
        functools.partial(_norm_body, has_delta=has_delta, emit_x=emit_x),
        grid=(m // tm,),
        in_specs=in_specs,
        out_specs=out_specs,
        out_shape=out_shape,
        compiler_params=_params(("parallel",)),
        name="rmsnorm",
    )(*args)
    return res if emit_x else res[0]


def _mm_body(*refs, nk, trans_b, epilogue, aliased):
    a_ref, b_ref = refs[0], refs[1]
    e_ref = refs[2] if epilogue else None
    n_in = 2 + bool(epilogue) + aliased
    o_ref = refs[n_in]
    acc_ref = refs[n_in + 1] if nk > 1 else None
    part = lax.dot_general(a_ref[...].astype(b_ref.dtype), b_ref[...], _NT if trans_b else _NN,
                           preferred_element_type=F32)

    def finish(acc):
        if epilogue == "residual":
            acc = acc + e_ref[...]
        elif epilogue == "glu":
            acc = e_ref[...].astype(F32) * _sigmoid(acc)
        o_ref[...] = acc.astype(o_ref.dtype)

    if nk == 1:
        finish(part)
        return
    k = pl.program_id(2)

    @pl.when(k == 0)
    def _():
        acc_ref[...] = part

    @pl.when(k > 0)
    def _():
        acc_ref[...] += part

    @pl.when(k == nk - 1)
    def _():
        finish(acc_ref[...])


def _matmul(a, b, *, trans_b=False, extra=None, epilogue=None, out_dtype=F32, into=None,
            tm=1024, tn=1024, tk=4096, name="matmul"):
    m, kdim = a.shape
    n = b.shape[0] if trans_b else b.shape[1]
    tm, tn, tk = math.gcd(tm, m), math.gcd(tn, n), math.gcd(tk, kdim)
    if into is not None:
        tn = math.gcd(tn, into[1])
    nk = kdim // tk
    a_spec = pl.BlockSpec((tm, tk), lambda i, j, k: (i, k))
    if trans_b:
        b_spec = pl.BlockSpec((tn, tk), lambda i, j, k: (j, k))
    else:
        b_spec = pl.BlockSpec((tk, tn), lambda i, j, k: (k, j))
    e_spec = pl.BlockSpec((tm, tn), lambda i, j, k: (i, j))
    in_specs = [a_spec, b_spec] + ([e_spec] if epilogue else [])
    args = [a, b] + ([extra] if epilogue else [])
    if into is None:
        o_spec, out_shape, aliases = e_spec, jax.ShapeDtypeStruct((m, n), out_dtype), {}
    else:
        dest, col = into
        joff = col // tn
        o_spec = pl.BlockSpec((tm, tn), lambda i, j, k: (i, j + joff))
        out_shape = jax.ShapeDtypeStruct(dest.shape, dest.dtype)
        aliases = {len(args): 0}
        in_specs = in_specs + [pl.BlockSpec(memory_space=pl.ANY)]
        args = args + [dest]
    scratch = [pltpu.VMEM((tm, tn), F32)] if nk > 1 else []
    return pl.pallas_call(
        functools.partial(_mm_body, nk=nk, trans_b=trans_b, epilogue=epilogue,
                          aliased=into is not None),
        grid=(m // tm, n // tn, nk),
        in_specs=in_specs,
        out_specs=o_spec,
        out_shape=out_shape,
        input_output_aliases=aliases,
        scratch_shapes=scratch,
        compiler_params=_params(("parallel", "parallel", "arbitrary")),
        name=name,
    )(*args)


HG_SUB = 16


def _hgrn2_body(lbp_ref, q_ref, f_ref, i_ref, g_ref, o_ref, st_ref, *, n_chunks, slot, hpb):
    @pl.when(pl.program_id(2) == 0)
    def _():
        st_ref[...] = jnp.zeros_like(st_ref)

    p = lbp_ref[...]
    ex = jnp.exp(p - jnp.max(p, axis=0, keepdims=True))
    sm = ex / jnp.sum(ex, axis=0, keepdims=True)
    lb = jnp.sum(sm[: slot + 1], axis=0, keepdims=True)

    ri = lax.broadcasted_iota(jnp.int32, (CHUNK, CHUNK), 0)
    ci = lax.broadcasted_iota(jnp.int32, (CHUNK, CHUNK), 1)
    tri = (ci <= ri).astype(F32)
    sub_row = lax.broadcasted_iota(jnp.int32, (HG_SUB, 1), 0)
    n_sub = CHUNK // HG_SUB
    heads_cols = [slice(hh * A_KEY, (hh + 1) * A_KEY) for hh in range(hpb)]

    def per_head(fn):
        return jnp.concatenate([fn(hh, cols) for hh, cols in enumerate(heads_cols)], axis=1)

    def head_sum(x):
        return per_head(lambda hh, cols: jnp.broadcast_to(
            jnp.sum(x[:, cols], axis=-1, keepdims=True), (x.shape[0], A_KEY)))

    def chunk(ci_, carry):
        rows = pl.ds(pl.multiple_of(ci_ * CHUNK, CHUNK), CHUNK)
        f = lb + (1.0 - lb) * _sigmoid(f_ref[rows, :])
        qr = q_ref[rows, :]
        q = qr * _sigmoid(qr)
        k = 1.0 - f
        v = i_ref[rows, :]
        cum = jnp.dot(tri, jnp.log(f), precision=lax.Precision.HIGHEST,
                      preferred_element_type=F32)
        v16 = v.astype(BF16)
        parts = []
        for sb in range(n_sub):
            lo = sb * HG_SUB
            cs = cum[lo:lo + HG_SUB]
            qs = q[lo:lo + HG_SUB]
            ks = k[lo:lo + HG_SUB]
            vs = v[lo:lo + HG_SUB]
            acc = jnp.zeros(cs.shape, F32)
            for s in range(HG_SUB):
                diff = jnp.where(sub_row >= s, cs - cs[s:s + 1], -jnp.inf)
                acc = acc + head_sum(qs * jnp.exp(diff) * ks[s:s + 1]) * vs[s:s + 1]
            if sb > 0:
                ref = cum[lo - 1:lo]
                qx = (qs * jnp.exp(cs - ref)).astype(BF16)
                kx = (k[:lo] * jnp.exp(ref - cum[:lo])).astype(BF16)

                def off_diag(hh, cols):
                    att = lax.dot_general(qx[:, cols], kx[:, cols], _NT,
                                          preferred_element_type=F32)
                    return jnp.dot(att.astype(BF16), v16[:lo, cols], preferred_element_type=F32)

                acc = acc + per_head(off_diag)
            parts.append(acc)
        o = jnp.concatenate(parts, axis=0)
        last = cum[CHUNK - 1:CHUNK]
        qd = (q * jnp.exp(cum)).astype(BF16)
        kd = (k * jnp.exp(last - cum)).astype(BF16)
        decay = jnp.exp(last)
        states = [st_ref[hh] for hh in range(hpb)]
        o = o + per_head(lambda hh, cols: lax.dot_general(
            qd[:, cols], states[hh].astype(BF16), _NT, preferred_element_type=F32))
        for hh, cols in enumerate(heads_cols):
            st_ref[hh] = states[hh] * decay[:, cols] + lax.dot_general(
                v16[:, cols], kd[:, cols], _TN, preferred_element_type=F32)
        o = o * lax.rsqrt(head_sum(o * o) * (1.0 / A_KEY) + EPS)
        gr = g_ref[rows, :]
        o_ref[rows, :] = (o * gr * _sigmoid(gr)).astype(o_ref.dtype)
        return carry

    lax.fori_loop(0, n_chunks, chunk, 0)


def _hgrn2(z, lb_param, slot, batch, seq, width, out_width, *, rows=512, heads_per_block=8):
    heads = width // A_KEY
    hpb = math.gcd(heads_per_block, heads)
    hblocks = heads // hpb
    rows = math.gcd(rows, seq)
    nblk = seq // rows
    lane = lambda off: pl.BlockSpec(
        (rows, hpb * A_KEY), lambda b, h, c, off=off: (b * nblk + c, off * hblocks + h))
    n_slots = lb_param.shape[0]
    return pl.pallas_call(
        functools.partial(_hgrn2_body, n_chunks=rows // CHUNK, slot=slot, hpb=hpb),
        grid=(batch, hblocks, nblk),
        in_specs=[pl.BlockSpec((n_slots, hpb * A_KEY), lambda b, h, c: (0, h)),
                  lane(0), lane(1), lane(2), lane(3)],
        out_specs=pl.BlockSpec((rows, hpb * A_KEY), lambda b, h, c: (b * nblk + c, h)),
        out_shape=jax.ShapeDtypeStruct((batch * seq, out_width), BF16),
        scratch_shapes=[pltpu.VMEM((hpb, A_KEY, A_KEY), F32)],
        compiler_params=_params(("parallel", "parallel", "arbitrary")),
        name="hgrn2",
    )(lb_param, z, z, z, z)


def _s5_tables(a_re, a_im, log_dt, b_re, b_im, c_re, c_im):
    hi = lax.Precision.HIGHEST
    L = S5_STEP
    groups, states = a_re.shape
    tiles = groups * B_GROUP // LANES
    gpt = LANES // B_GROUP
    dt = jnp.exp(log_dt.astype(F32))[:, None]
    lam_re = jnp.minimum(a_re.astype(F32), S5_DT_FLOOR)
    lam_im = a_im.astype(F32)
    mag = jnp.exp(lam_re * dt)
    ang = lam_im * dt
    ab_re = mag * jnp.cos(ang)
    ab_im = mag * jnp.sin(ang)
    nr = ab_re - 1.0
    ni = ab_im
    den = lam_re * lam_re + lam_im * lam_im
    z_re = (nr * lam_re + ni * lam_im) / den
    z_im = (ni * lam_re - nr * lam_im) / den
    br = b_re.astype(F32)
    bi = b_im.astype(F32)
    bb_re = z_re[..., None] * br - z_im[..., None] * bi
    bb_im = z_re[..., None] * bi + z_im[..., None] * br
    pw_re, pw_im = [jnp.ones_like(ab_re)], [jnp.zeros_like(ab_re)]
    for _ in range(L):
        pr, pi = pw_re[-1], pw_im[-1]
        pw_re.append(pr * ab_re - pi * ab_im)
        pw_im.append(pr * ab_im + pi * ab_re)
    pw_re = jnp.stack(pw_re)
    pw_im = jnp.stack(pw_im)
    cr = c_re.astype(F32)[None]
    ci = c_im.astype(F32)[None]
    cp_re = cr * pw_re[:, :, None, :] - ci * pw_im[:, :, None, :]
    cp_im = cr * pw_im[:, :, None, :] + ci * pw_re[:, :, None, :]
    lag = (jnp.einsum("lghp,gpk->lghk", cp_re[:L], bb_re, precision=hi)
           - jnp.einsum("lghp,gpk->lghk", cp_im[:L], bb_im, precision=hi))
    def embed(compact, inner, row_group):
        cols = compact.shape[-1]
        wide_col = jnp.arange(cols * gpt)
        rep = jnp.arange(cols)[:, None] == (wide_col // (gpt * inner)) * inner + wide_col % inner
        wide = jnp.matmul(compact, rep.astype(BF16), preferred_element_type=F32)
        keep = row_group[:, None] == ((wide_col // inner) % gpt)[None, :]
        return jnp.where(keep, wide, 0.0).astype(BF16)

    key_group = jnp.arange(LANES) // B_GROUP
    state_group = (jnp.arange(2 * gpt * states) // states) % gpt
    kc = lag.reshape(L, tiles, gpt, B_GROUP, B_GROUP).astype(BF16)
    kc = kc.transpose(1, 2, 4, 0, 3).reshape(tiles, LANES, L * B_GROUP)
    kcat = embed(kc, B_GROUP, key_group)
    rev = slice(L - 1, None, -1)
    wb_re = pw_re[rev][..., None] * bb_re[None] - pw_im[rev][..., None] * bb_im[None]
    wb_im = pw_re[rev][..., None] * bb_im[None] + pw_im[rev][..., None] * bb_re[None]
    wb = jnp.stack([wb_re, wb_im], axis=1).astype(BF16)
    wb = wb.reshape(L, 2, tiles, gpt, states, B_GROUP).transpose(2, 0, 3, 5, 1, 4)
    wst = embed(wb.reshape(tiles, L, LANES, 2 * states), states, key_group)
    mo = jnp.stack([cp_re[1:], -cp_im[1:]], axis=1).astype(BF16)
    mo = mo.reshape(L, 2, tiles, gpt, B_GROUP, states).transpose(2, 1, 3, 5, 0, 4)
    mst = embed(mo.reshape(tiles, 2 * gpt * states, L * B_GROUP), B_GROUP, state_group)
    al = jnp.stack([pw_re[L].reshape(tiles, gpt * states),
                    pw_im[L].reshape(tiles, gpt * states)], axis=1)
    return kcat, wst, mst, al


def _s5_body(u_ref, kc_ref, ws_ref, ms_ref, al_ref, d_ref, o_ref, sl_ref, x_ref, y_ref, *, nc):
    L = S5_STEP
    half = sl_ref.shape[1] // 2
    step_rows = lambda s: u_ref[pl.ds(s, nc, stride=L), :]
    for s in range(L):
        part = jnp.dot(step_rows(s).astype(BF16), ws_ref[s], preferred_element_type=F32)
        if s == 0:
            sl_ref[...] = part
        else:
            sl_ref[...] += part
    alr = al_ref[0:1, :]
    ali = al_ref[1:2, :]

    def step(c, carry):
        xr, xi = carry
        row = pl.ds(c, 1)
        x_ref[row, 0:half] = xr
        x_ref[row, half:] = xi
        sr = sl_ref[row, 0:half]
        si = sl_ref[row, half:]
        return alr * xr - ali * xi + sr, alr * xi + ali * xr + si

    zero = jnp.zeros((1, half), F32)
    lax.fori_loop(0, nc, step, (zero, zero))
    y_ref[...] = jnp.dot(x_ref[...].astype(BF16), ms_ref[...], preferred_element_type=F32)
    for s in range(L):
        y_ref[:, s * LANES:] += jnp.dot(step_rows(s).astype(BF16),
                                        kc_ref[:, :(L - s) * LANES],
                                        preferred_element_type=F32)
    for t in range(L):
        y = y_ref[:, t * LANES:(t + 1) * LANES] + d_ref[...] * step_rows(t)
        o_ref[pl.ds(t, nc, stride=L), :] = _gelu(y)


def _s5(z, col_off, tables, d_skip, batch, seq, width):
    kcat, wst, mst, al = tables
    L = S5_STEP
    tiles = width // LANES
    nc = seq // L
    nstate = wst.shape[-1]
    off = col_off // LANES
    return pl.pallas_call(
        functools.partial(_s5_body, nc=nc),
        grid=(tiles, batch),
        in_specs=[
            pl.BlockSpec((seq, LANES), lambda j, b: (b, off + j)),
            pl.BlockSpec((None, LANES, L * LANES), lambda j, b: (j, 0, 0)),
            pl.BlockSpec((None, L, LANES, nstate), lambda j, b: (j, 0, 0, 0)),
            pl.BlockSpec((None, nstate, L * LANES), lambda j, b: (j, 0, 0)),
            pl.BlockSpec((None, 2, nstate // 2), lambda j, b: (j, 0, 0)),
            pl.BlockSpec((None, 1, LANES), lambda j, b: (j, 0, 0)),
        ],
        out_specs=pl.BlockSpec((seq, LANES), lambda j, b: (b, j)),
        out_shape=jax.ShapeDtypeStruct((batch * seq, width), F32),
        scratch_shapes=[pltpu.VMEM((nc, nstate), F32), pltpu.VMEM((nc, nstate), F32),
                        pltpu.VMEM((nc, L * LANES), F32)],
        compiler_params=_params(("parallel", "arbitrary")),
        name="s5",
    )(z, kcat, wst, mst, al, d_skip.astype(F32).reshape(tiles, 1, LANES))


def _ret_body(lg_ref, q_ref, k_ref, v_ref, g_ref, cos_ref, sin_ref, o_ref, r_ref, dec_ref,
              *, dk, dv, hp):
    n = RET_BLOCK
    head0 = pl.program_id(1) * hp

    @pl.when(pl.program_id(2) == 0)
    def _():
        r_ref[...] = jnp.zeros_like(r_ref)
        ti = lax.broadcasted_iota(jnp.int32, (n, n), 0)
        si = lax.broadcasted_iota(jnp.int32, (n, n), 1)
        dist = jnp.abs(ti - si).astype(F32)
        for i in range(hp):
            dec_ref[i] = jnp.where(si // CHUNK <= ti // CHUNK,
                                   jnp.exp(dist * lg_ref[head0 + i]), 0.0)

    cos = cos_ref[...]
    sin = sin_ref[...]
    hd = dk // 2
    pos = lax.broadcasted_iota(jnp.int32, (n, 1), 0).astype(F32)

    def rot(t):
        t1, t2 = t[:, :hd], t[:, hd:]
        return jnp.concatenate([t1 * cos - t2 * sin, t2 * cos + t1 * sin], axis=1)

    for i in range(hp):
        lg = lg_ref[head0 + i]
        qk_cols = slice(i * dk, (i + 1) * dk)
        v_cols = slice(i * dv, (i + 1) * dv)
        q = rot(q_ref[:, qk_cols].astype(F32))
        k = rot(k_ref[:, qk_cols].astype(F32)) * (dk ** -0.5)
        sc = lax.dot_general(q.astype(BF16), k.astype(BF16), _NT,
                             preferred_element_type=F32) * dec_ref[i]
        v = v_ref[:, v_cols]
        qd = jnp.exp((pos + 1.0) * lg)
        kd = jnp.exp((n - 1.0 - pos) * lg)
        r = r_ref[i]
        o = (jnp.dot(sc.astype(BF16), v, preferred_element_type=F32)
             + jnp.dot((q * qd).astype(BF16), r.astype(BF16), preferred_element_type=F32))
        r_ref[i] = jnp.exp(n * lg) * r + lax.dot_general((k * kd).astype(BF16), v, _TN,
                                                         preferred_element_type=F32)
        o = o * lax.rsqrt(jnp.mean(o * o, axis=-1, keepdims=True) + EPS)
        g = g_ref[:, v_cols].astype(F32)
        o_ref[:, v_cols] = (o * g * _sigmoid(g)).astype(o_ref.dtype)


def _retention(z, batch, seq, d_qk, d_v):
    dk = d_qk // C_HEADS
    dv = d_v // C_HEADS
    n = RET_BLOCK
    nblk = seq // n
    pos = jnp.arange(seq, dtype=F32)
    inv_freq = ROPE_BASE ** (-jnp.arange(0, dk, 2, dtype=F32) / dk)
    ang = pos[:, None] * inv_freq[None, :]
    log_gamma = jnp.log(1.0 - 2.0 ** (-5.0 - jnp.arange(C_HEADS, dtype=F32)))
    hp = math.gcd(RET_HEADS_PER_STEP, C_HEADS)
    hblocks = C_HEADS // hp
    wk, wv = hp * dk, hp * dv
    row = lambda b, h, c, lg: b * nblk + c
    grid_spec = pltpu.PrefetchScalarGridSpec(
        num_scalar_prefetch=1,
        grid=(batch, hblocks, nblk),
        in_specs=[
            pl.BlockSpec((n, wk), lambda b, h, c, lg: (row(b, h, c, lg), h)),
            pl.BlockSpec((n, wk), lambda b, h, c, lg: (row(b, h, c, lg), hblocks + h)),
            pl.BlockSpec((n, wv), lambda b, h, c, lg: (row(b, h, c, lg), 2 * d_qk // wv + h)),
            pl.BlockSpec((n, wv), lambda b, h, c, lg: (row(b, h, c, lg),
                                                       (2 * d_qk + d_v) // wv + h)),
            pl.BlockSpec((n, dk // 2), lambda b, h, c, lg: (c, 0)),
            pl.BlockSpec((n, dk // 2), lambda b, h, c, lg: (c, 0)),
        ],
        out_specs=pl.BlockSpec((n, wv), lambda b, h, c, lg: (row(b, h, c, lg), h)),
        scratch_shapes=[pltpu.VMEM((hp, dk, dv), F32), pltpu.VMEM((hp, n, n), F32)],
    )
    return pl.pallas_call(
        functools.partial(_ret_body, dk=dk, dv=dv, hp=hp),
        grid_spec=grid_spec,
        out_shape=jax.ShapeDtypeStruct((batch * seq, d_v), BF16),
        compiler_params=_params(("parallel", "parallel", "arbitrary")),
        name="retention",
    )(log_gamma, z, z, z, z, jnp.cos(ang), jnp.sin(ang))


def _fold_body(k_ref, w_ref, o_ref):
    o_ref[...] = lax.dot_general(k_ref[...].astype(BF16), w_ref[...].astype(BF16), _NT,
                                 preferred_element_type=F32).astype(o_ref.dtype)


def _peer_fold(sub_keys, w_q, *, td=1024):
    heads, two, nk, kh = sub_keys.shape
    d = w_q.shape[0]
    td = min(td, d)
    return pl.pallas_call(
        _fold_body,
        grid=(heads * two, d // td),
        in_specs=[pl.BlockSpec((nk, kh), lambda hp, j: (hp, 0)),
                  pl.BlockSpec((td, kh), lambda hp, j: (j, hp))],
        out_specs=pl.BlockSpec((nk, td), lambda hp, j: (hp, j)),
        out_shape=jax.ShapeDtypeStruct((heads * two * nk, d), BF16),
        compiler_params=_params(("parallel", "parallel")),
        name="peer_fold",
    )(sub_keys.reshape(heads * two * nk, kh), w_q)


def _top_k_rows(s, k):
    rows = lax.broadcasted_iota(jnp.int32, s.shape, 0).astype(F32)
    big = float(s.shape[0])
    rank = jnp.full(s.shape, float(k), F32)
    vals = []
    for i in range(k):
        m = jnp.max(s, axis=0, keepdims=True)
        first = jnp.min(jnp.where(s == m, rows, big), axis=0, keepdims=True)
        vals.append(m)
        hit = rows == first
        rank = jnp.where(hit, float(i), rank)
        s = jnp.where(hit, -jnp.inf, s)
    return jnp.concatenate(vals, axis=0), rank


def _peer_topk_body(sc_ref, e1_ref, e2_ref):
    nk = PEER_NKEYS
    k = PEER_TOPK
    s1 = sc_ref[0:nk, :]
    s2 = sc_ref[nk:2 * nk, :]
    t1, rank1 = _top_k_rows(s1, k)
    t2, rank2 = _top_k_rows(s2, k)
    sub = lax.broadcasted_iota(jnp.int32, (SUBLANES, 1), 0)
    blocks = [t1[0:1] + t2]
    for a in range(1, k):
        blocks.append(jnp.where(sub < k // (a + 1), t1[a:a + 1] + t2[0:SUBLANES], -jnp.inf))
    top, crank = _top_k_rows(jnp.concatenate(blocks, axis=0), k)
    chosen = (crank < float(k)).astype(F32)
    cnt_a = [jnp.sum(chosen[0:k], axis=0, keepdims=True)]
    for a in range(1, k):
        lo = k + (a - 1) * SUBLANES
        cnt_a.append(jnp.sum(chosen[lo:lo + SUBLANES], axis=0, keepdims=True))
    cnt = jnp.zeros(s1.shape, F32)
    for a in range(k):
        cnt = cnt + jnp.where(rank1 == float(a), cnt_a[a], 0.0)
    z = jnp.sum(jnp.exp(top - top[0:1]), axis=0, keepdims=True)
    e1_ref[0:nk, :] = jnp.exp(s1 - t1[0:1]) / z
    e1_ref[nk:2 * nk, :] = cnt
    e2_ref[0:nk, :] = jnp.exp(s2 - t2[0:1])
    e2_ref[nk:2 * nk, :] = rank2


def _peer_topk(sc_t, heads, *, tt=256):
    rows, t = sc_t.shape
    per = rows // heads
    tt = math.gcd(tt, t)
    spec = pl.BlockSpec((per, tt), lambda i, h: (h, i))
    return pl.pallas_call(
        _peer_topk_body,
        grid=(t // tt, heads),
        in_specs=[spec],
        out_specs=[spec, spec],
        out_shape=[jax.ShapeDtypeStruct((rows, t), F32)] * 2,
        compiler_params=_params(("parallel", "parallel")),
        name="peer_topk",
    )(sc_t)


def _peer_main_body(xn_ref, u_ref, v_ref, e1_ref, e2_ref, o_ref, p_ref, w_ref, *, heads):
    j = pl.program_id(1)

    @pl.when(j == 0)
    def _():
        o_ref[...] = jnp.zeros_like(o_ref)

    nk = PEER_NKEYS
    te, tt = p_ref.shape
    n_e1 = te // nk
    sub = lax.broadcasted_iota(jnp.int32, (SUBLANES, 1), 0)

    def key_row(row):
        start = pl.multiple_of((row // SUBLANES) * SUBLANES, SUBLANES)
        grp = e1_ref[pl.ds(start, SUBLANES), :]
        return jnp.sum(jnp.where(sub == row % SUBLANES, grp, 0.0), axis=0, keepdims=True)

    for r in range(n_e1):
        e1 = j * n_e1 + r
        rows = slice(r * nk, (r + 1) * nk)
        a1 = [key_row(h * 2 * nk + e1) for h in range(heads)]
        cnt = [key_row(h * 2 * nk + nk + e1) for h in range(heads)]
        for c in range(tt // LANES):
            lanes = slice(c * LANES, (c + 1) * LANES)
            w = jnp.zeros((nk, LANES), F32)
            for h in range(heads):
                base = h * 2 * nk
                a2 = e2_ref[base:base + nk, lanes]
                rank2 = e2_ref[base + nk:base + 2 * nk, lanes]
                w = w + jnp.where(rank2 < cnt[h][:, lanes], a2 * a1[h][:, lanes], 0.0)
            w_ref[rows, lanes] = w
    act = _gelu(lax.dot_general(u_ref[...], xn_ref[...], _NT, preferred_element_type=F32))
    p_ref[...] = (act * w_ref[...]).astype(p_ref.dtype)
    o_ref[...] += lax.dot_general(p_ref[...], v_ref[...], _TN, preferred_element_type=F32)


def _peer_main(xn, u16, v16, layer, e1_t, e2_t, heads, *, tt=512, te=512):
    t, d = xn.shape
    n_exp = PEER_NKEYS * PEER_NKEYS
    tt = min(tt, t)
    rows = e1_t.shape[0]
    tile0 = layer * (n_exp // te)
    once = dict(pipeline_mode=pl.Buffered(1))
    return pl.pallas_call(
        functools.partial(_peer_main_body, heads=heads),
        grid=(t // tt, n_exp // te),
        in_specs=[
            pl.BlockSpec((tt, d), lambda i, j: (i, 0), **once),
            pl.BlockSpec((te, d), lambda i, j: (tile0 + j, 0)),
            pl.BlockSpec((te, d), lambda i, j: (tile0 + j, 0)),
            pl.BlockSpec((rows, tt), lambda i, j: (0, i), **once),
            pl.BlockSpec((rows, tt), lambda i, j: (0, i), **once),
        ],
        out_specs=pl.BlockSpec((tt, d), lambda i, j: (i, 0)),
        out_shape=jax.ShapeDtypeStruct((t, d), F32),
        scratch_shapes=[pltpu.VMEM((te, tt), BF16), pltpu.VMEM((te, tt), F32)],
        compiler_params=_params(("parallel", "arbitrary")),
        name="peer_main",
    )(xn, u16, v16, e1_t, e2_t)


def _peer(xn, w_q, sub_keys, u16, v16, layer):
    heads = sub_keys.shape[0]
    fold = _peer_fold(sub_keys, w_q)
    sc_t = _matmul(fold, xn, trans_b=True, name="peer_scores")
    e1_t, e2_t = _peer_topk(sc_t, heads)
    return _peer_main(xn, u16, v16, layer, e1_t, e2_t, heads)


def kernel(x, mix_norm_g, ab_w_in, a_lb_param, b_a_re, b_a_im, b_log_dt, b_b_re, b_b_im, b_c_re, b_c_im, b_d, b_w_glu, ab_w_out, c_w_in, c_w_out, ffn_norm_g, peer_w_q, peer_sub_keys, peer_u, peer_v, final_norm_g):
    batch, seq, d = x.shape
    depth = mix_norm_g.shape[0]
    xs = x.reshape(batch * seq, d)
    u16 = peer_u.astype(BF16).reshape(-1, d)
    v16 = peer_v.astype(BF16).reshape(-1, d)
    delta = None
    for layer in range(depth):
        j = layer // 2
        if delta is None:
            h = _rmsnorm(xs, mix_norm_g[layer])
        else:
            xs, h = _rmsnorm(xs, mix_norm_g[layer], delta, emit_x=True)
        if layer % 2 == 0:
            a_width = a_lb_param.shape[1]
            b_width = b_w_glu.shape[1]
            z = _matmul(h, ab_w_in[j].astype(BF16), name="ab_in")
            o = _hgrn2(z, a_lb_param.astype(F32), j, batch, seq, a_width, a_width + b_width)
            tables = _s5_tables(b_a_re[j], b_a_im[j], b_log_dt[j], b_b_re[j], b_b_im[j],
                                b_c_re[j], b_c_im[j])
            yg = _s5(z, 4 * a_width, tables, b_d[j], batch, seq, b_width)
            o = _matmul(yg, b_w_glu[j].astype(BF16), extra=yg, epilogue="glu",
                        into=(o, a_width), name="s5_glu")
            xs = _matmul(o, ab_w_out[j].astype(BF16), extra=xs, epilogue="residual",
                         tm=512, name="ab_out")
        else:
            d_v = c_w_out.shape[1]
            d_qk = (c_w_in.shape[2] - 2 * d_v) // 2
            z = _matmul(h, c_w_in[j].astype(BF16), out_dtype=BF16, name="ret_in")
            o = _retention(z, batch, seq, d_qk, d_v)
            xs = _matmul(o, c_w_out[j].astype(BF16), extra=xs, epilogue="residual",
                         tm=512, tn=512, tk=8192, name="ret_out")
        hn = _rmsnorm(xs, ffn_norm_g[layer])
        delta = _peer(hn, peer_w_q[layer], peer_sub_keys[layer], u16, v16, layer)
    out = _rmsnorm(xs, final_norm_g, delta, out_dtype=F32)
    return out.reshape(batch, seq, d)
```

```python
import functools
import math

import jax
import jax.numpy as jnp
from jax import lax
from jax.experimental import pallas as pl
from jax.experimental.pallas import tpu as pltpu

EPS = 1e-6
CHUNK = 64
A_KEY = 128
B_GROUP = 16
B_STATE = 64
S5_DT_FLOOR = -1e-4
S5_STEP = 16
C_HEADS = 16
ROPE_BASE = 10000.0
RET_BLOCK = 4 * CHUNK
RET_HEADS_PER_STEP = 2
PEER_TOPK = 16
PEER_NKEYS = 128

LANES = 128
SUBLANES = 8
V7X_VMEM_LIMIT = 56 * 1024 * 1024

F32 = jnp.float32
BF16 = jnp.bfloat16

_NN = (((1,), (0,)), ((), ()))
_NT = (((1,), (1,)), ((), ()))
_TN = (((0,), (0,)), ((), ()))


def _params(sem):
    return pltpu.CompilerParams(dimension_semantics=sem, vmem_limit_bytes=V7X_VMEM_LIMIT)


def _sigmoid(x):
    return 1.0 / (1.0 + jnp.exp(-x))


def _gelu(x):
    return 0.5 * x * (1.0 + lax.erf(x * math.sqrt(0.5)))


def _norm_body(*refs, has_delta, emit_x):
    x_ref = refs[0]
    d_ref = refs[1] if has_delta else None
    g_ref = refs[1 + has_delta]
    outs = refs[2 + has_delta:]
    x = x_ref[...]
    if has_delta:
        x = x + d_ref[...]
    if emit_x:
        outs[0][...] = x
    h_ref = outs[-1]
    ms = jnp.mean(x * x, axis=-1, keepdims=True)
    h_ref[...] = (x * lax.rsqrt(ms + EPS) * g_ref[...]).astype(h_ref.dtype)


def _rmsnorm(x, g, delta=None, *, emit_x=False, out_dtype=BF16, tm=256):
    m, d = x.shape
    has_delta = delta is not None
    row = pl.BlockSpec((tm, d), lambda i: (i, 0))
    in_specs = [row] + ([row] if has_delta else []) + [pl.BlockSpec((1, d), lambda i: (0, 0))]
    args = [x] + ([delta] if has_delta else []) + [g.reshape(1, d)]
    out_shape = [jax.ShapeDtypeStruct((m, d), out_dtype)]
    out_specs = [row]
    if emit_x:
        out_shape = [jax.ShapeDtypeStruct((m, d), F32)] + out_shape
        out_specs = [row] + out_specs
    res = pl.pallas_call(
        functools.partial(_norm_body, has_delta=has_delta, emit_x=emit_x),
        grid=(m // tm,),
        in_specs=in_specs,
        out_specs=out_specs,
        out_shape=out_shape,
        compiler_params=_params(("parallel",)),
        name="rmsnorm",
    )(*args)
    return res if emit_x else res[0]


def _mm_body(*refs, nk, trans_b, epilogue, aliased):
    a_ref, b_ref = refs[0], refs[1]
    e_ref = refs[2] if epilogue else None
    n_in = 2 + bool(epilogue) + aliased
    o_ref = refs[n_in]
    acc_ref = refs[n_in + 1] if nk > 1 else None
    part = lax.dot_general(a_ref[...].astype(b_ref.dtype), b_ref[...], _NT if trans_b else _NN,
                           preferred_element_type=F32)

    def finish(acc):
        if epilogue == "residual":
            acc = acc + e_ref[...]
        elif epilogue == "glu":
            acc = e_ref[...].astype(F32) * _sigmoid(acc)
        o_ref[...] = acc.astype(o_ref.dtype)

    if nk == 1:
        finish(part)
        return
    k = pl.program_id(2)

    @pl.when(k == 0)
    def _():
        acc_ref[...] = part

    @pl.when(k > 0)
    def _():
        acc_ref[...] += part

    @pl.when(k == nk - 1)
    def _():
        finish(acc_ref[...])


def _matmul(a, b, *, trans_b=False, extra=None, epilogue=None, out_dtype=F32, into=None,
            tm=1024, tn=1024, tk=4096, name="matmul"):
    m, kdim = a.shape
    n = b.shape[0] if trans_b else b.shape[1]
    tm, tn, tk = math.gcd(tm, m), math.gcd(tn, n), math.gcd(tk, kdim)
    if into is not None:
        tn = math.gcd(tn, into[1])
    nk = kdim // tk
    a_spec = pl.BlockSpec((tm, tk), lambda i, j, k: (i, k))
    if trans_b:
        b_spec = pl.BlockSpec((tn, tk), lambda i, j, k: (j, k))
    else:
        b_spec = pl.BlockSpec((tk, tn), lambda i, j, k: (k, j))
    e_spec = pl.BlockSpec((tm, tn), lambda i, j, k: (i, j))
    in_specs = [a_spec, b_spec] + ([e_spec] if epilogue else [])
    args = [a, b] + ([extra] if epilogue else [])
    if into is None:
        o_spec, out_shape, aliases = e_spec, jax.ShapeDtypeStruct((m, n), out_dtype), {}
    else:
        dest, col = into
        joff = col // tn
        o_spec = pl.BlockSpec((tm, tn), lambda i, j, k: (i, j + joff))
        out_shape = jax.ShapeDtypeStruct(dest.shape, dest.dtype)
        aliases = {len(args): 0}
        in_specs = in_specs + [pl.BlockSpec(memory_space=pl.ANY)]
        args = args + [dest]
    scratch = [pltpu.VMEM((tm, tn), F32)] if nk > 1 else []
    return pl.pallas_call(
        functools.partial(_mm_body, nk=nk, trans_b=trans_b, epilogue=epilogue,
                          aliased=into is not None),
        grid=(m // tm, n // tn, nk),
        in_specs=in_specs,
        out_specs=o_spec,
        out_shape=out_shape,
        input_output_aliases=aliases,
        scratch_shapes=scratch,
        compiler_params=_params(("parallel", "parallel", "arbitrary")),
        name=name,
    )(*args)


HG_SUB = 16


def _hgrn2_body(lbp_ref, q_ref, f_ref, i_ref, g_ref, o_ref, st_ref, *, n_chunks, slot, hpb):
    @pl.when(pl.program_id(2) == 0)
    def _():
        st_ref[...] = jnp.zeros_like(st_ref)

    p = lbp_ref[...]
    ex = jnp.exp(p - jnp.max(p, axis=0, keepdims=True))
    sm = ex / jnp.sum(ex, axis=0, keepdims=True)
    lb = jnp.sum(sm[: slot + 1], axis=0, keepdims=True)

    ri = lax.broadcasted_iota(jnp.int32, (CHUNK, CHUNK), 0)
    ci = lax.broadcasted_iota(jnp.int32, (CHUNK, CHUNK), 1)
    tri = (ci <= ri).astype(F32)
    sub_row = lax.broadcasted_iota(jnp.int32, (HG_SUB, 1), 0)
    n_sub = CHUNK // HG_SUB
    heads_cols = [slice(hh * A_KEY, (hh + 1) * A_KEY) for hh in range(hpb)]

    def per_head(fn):
        return jnp.concatenate([fn(hh, cols) for hh, cols in enumerate(heads_cols)], axis=1)

    def head_sum(x):
        return per_head(lambda hh, cols: jnp.broadcast_to(
            jnp.sum(x[:, cols], axis=-1, keepdims=True), (x.shape[0], A_KEY)))

    def chunk(ci_, carry):
        rows = pl.ds(pl.multiple_of(ci_ * CHUNK, CHUNK), CHUNK)
        f = lb + (1.0 - lb) * _sigmoid(f_ref[rows, :])
        qr = q_ref[rows, :]
        q = qr * _sigmoid(qr)
        k = 1.0 - f
        v = i_ref[rows, :]
        cum = jnp.dot(tri, jnp.log(f), precision=lax.Precision.HIGHEST,
                      preferred_element_type=F32)
        v16 = v.astype(BF16)
        parts = []
        for sb in range(n_sub):
            lo = sb * HG_SUB
            cs = cum[lo:lo + HG_SUB]
            qs = q[lo:lo + HG_SUB]
            ks = k[lo:lo + HG_SUB]
            vs = v[lo:lo + HG_SUB]
            acc = jnp.zeros(cs.shape, F32)
            for s in range(HG_SUB):
                diff = jnp.where(sub_row >= s, cs - cs[s:s + 1], -jnp.inf)
                acc = acc + head_sum(qs * jnp.exp(diff) * ks[s:s + 1]) * vs[s:s + 1]
            if sb > 0:
                ref = cum[lo - 1:lo]
                qx = (qs * jnp.exp(cs - ref)).astype(BF16)
                kx = (k[:lo] * jnp.exp(ref - cum[:lo])).astype(BF16)

                def off_diag(hh, cols):
                    att = lax.dot_general(qx[:, cols], kx[:, cols], _NT,
                                          preferred_element_type=F32)
                    return jnp.dot(att.astype(BF16), v16[:lo, cols], preferred_element_type=F32)

                acc = acc + per_head(off_diag)
            parts.append(acc)
        o = jnp.concatenate(parts, axis=0)
        last = cum[CHUNK - 1:CHUNK]
        qd = (q * jnp.exp(cum)).astype(BF16)
        kd = (k * jnp.exp(last - cum)).astype(BF16)
        decay = jnp.exp(last)
        states = [st_ref[hh] for hh in range(hpb)]
        o = o + per_head(lambda hh, cols: lax.dot_general(
            qd[:, cols], states[hh].astype(BF16), _NT, preferred_element_type=F32))
        for hh, cols in enumerate(heads_cols):
            st_ref[hh] = states[hh] * decay[:, cols] + lax.dot_general(
                v16[:, cols], kd[:, cols], _TN, preferred_element_type=F32)
        o = o * lax.rsqrt(head_sum(o * o) * (1.0 / A_KEY) + EPS)
        gr = g_ref[rows, :]
        o_ref[rows, :] = (o * gr * _sigmoid(gr)).astype(o_ref.dtype)
        return carry

    lax.fori_loop(0, n_chunks, chunk, 0)


def _hgrn2(z, lb_param, slot, batch, seq, width, out_width, *, rows=512, heads_per_block=8):
    heads = width // A_KEY
    hpb = math.gcd(heads_per_block, heads)
    hblocks = heads // hpb
    rows = math.gcd(rows, seq)
    nblk = seq // rows
    lane = lambda off: pl.BlockSpec(
        (rows, hpb * A_KEY), lambda b, h, c, off=off: (b * nblk + c, off * hblocks + h))
    n_slots = lb_param.shape[0]
    return pl.pallas_call(
        functools.partial(_hgrn2_body, n_chunks=rows // CHUNK, slot=slot, hpb=hpb),
        grid=(batch, hblocks, nblk),
        in_specs=[pl.BlockSpec((n_slots, hpb * A_KEY), lambda b, h, c: (0, h)),
                  lane(0), lane(1), lane(2), lane(3)],
        out_specs=pl.BlockSpec((rows, hpb * A_KEY), lambda b, h, c: (b * nblk + c, h)),
        out_shape=jax.ShapeDtypeStruct((batch * seq, out_width), BF16),
        scratch_shapes=[pltpu.VMEM((hpb, A_KEY, A_KEY), F32)],
        compiler_params=_params(("parallel", "parallel", "arbitrary")),
        name="hgrn2",
    )(lb_param, z, z, z, z)


def _s5_tables(a_re, a_im, log_dt, b_re, b_im, c_re, c_im):
    hi = lax.Precision.HIGHEST
    L = S5_STEP
    groups, states = a_re.shape
    tiles = groups * B_GROUP // LANES
    gpt = LANES // B_GROUP
    dt = jnp.exp(log_dt.astype(F32))[:, None]
    lam_re = jnp.minimum(a_re.astype(F32), S5_DT_FLOOR)
    lam_im = a_im.astype(F32)
    mag = jnp.exp(lam_re * dt)
    ang = lam_im * dt
    ab_re = mag * jnp.cos(ang)
    ab_im = mag * jnp.sin(ang)
    nr = ab_re - 1.0
    ni = ab_im
    den = lam_re * lam_re + lam_im * lam_im
    z_re = (nr * lam_re + ni * lam_im) / den
    z_im = (ni * lam_re - nr * lam_im) / den
    br = b_re.astype(F32)
    bi = b_im.astype(F32)
    bb_re = z_re[..., None] * br - z_im[..., None] * bi
    bb_im = z_re[..., None] * bi + z_im[..., None] * br
    pw_re, pw_im = [jnp.ones_like(ab_re)], [jnp.zeros_like(ab_re)]
    for _ in range(L):
        pr, pi = pw_re[-1], pw_im[-1]
        pw_re.append(pr * ab_re - pi * ab_im)
        pw_im.append(pr * ab_im + pi * ab_re)
    pw_re = jnp.stack(pw_re)
    pw_im = jnp.stack(pw_im)
    cr = c_re.astype(F32)[None]
    ci = c_im.astype(F32)[None]
    cp_re = cr * pw_re[:, :, None, :] - ci * pw_im[:, :, None, :]
    cp_im = cr * pw_im[:, :, None, :] + ci * pw_re[:, :, None, :]
    lag = (jnp.einsum("lghp,gpk->lghk", cp_re[:L], bb_re, precision=hi)
           - jnp.einsum("lghp,gpk->lghk", cp_im[:L], bb_im, precision=hi))
    def embed(compact, inner, row_group):
        cols = compact.shape[-1]
        wide_col = jnp.arange(cols * gpt)
        rep = jnp.arange(cols)[:, None] == (wide_col // (gpt * inner)) * inner + wide_col % inner
        wide = jnp.matmul(compact, rep.astype(BF16), preferred_element_type=F32)
        keep = row_group[:, None] == ((wide_col // inner) % gpt)[None, :]
        return jnp.where(keep, wide, 0.0).astype(BF16)

    key_group = jnp.arange(LANES) // B_GROUP
    state_group = (jnp.arange(2 * gpt * states) // states) % gpt
    kc = lag.reshape(L, tiles, gpt, B_GROUP, B_GROUP).astype(BF16)
    kc = kc.transpose(1, 2, 4, 0, 3).reshape(tiles, LANES, L * B_GROUP)
    kcat = embed(kc, B_GROUP, key_group)
    rev = slice(L - 1, None, -1)
    wb_re = pw_re[rev][..., None] * bb_re[None] - pw_im[rev][..., None] * bb_im[None]
    wb_im = pw_re[rev][..., None] * bb_im[None] + pw_im[rev][..., None] * bb_re[None]
    wb = jnp.stack([wb_re, wb_im], axis=1).astype(BF16)
    wb = wb.reshape(L, 2, tiles, gpt, states, B_GROUP).transpose(2, 0, 3, 5, 1, 4)
    wst = embed(wb.reshape(tiles, L, LANES, 2 * states), states, key_group)
    mo = jnp.stack([cp_re[1:], -cp_im[1:]], axis=1).astype(BF16)
    mo = mo.reshape(L, 2, tiles, gpt, B_GROUP, states).transpose(2, 1, 3, 5, 0, 4)
    mst = embed(mo.reshape(tiles, 2 * gpt * states, L * B_GROUP), B_GROUP, state_group)
    al = jnp.stack([pw_re[L].reshape(tiles, gpt * states),
                    pw_im[L].reshape(tiles, gpt * states)], axis=1)
    return kcat, wst, mst, al


def _s5_body(u_ref, kc_ref, ws_ref, ms_ref, al_ref, d_ref, o_ref, sl_ref, x_ref, y_ref, *, nc):
    L = S5_STEP
    half = sl_ref.shape[1] // 2
    step_rows = lambda s: u_ref[pl.ds(s, nc, stride=L), :]
    frames = jnp.concatenate([step_rows(s).astype(BF16) for s in range(L)], axis=1)
    sl_ref[...] = jnp.dot(frames, ws_ref[...].reshape(L * LANES, 2 * half),
                          preferred_element_type=F32)
    alr = al_ref[0:1, :]
    ali = al_ref[1:2, :]

    def step(c, carry):
        xr, xi = carry
        row = pl.ds(c, 1)
        x_ref[row, 0:half] = xr
        x_ref[row, half:] = xi
        sr = sl_ref[row, 0:half]
        si = sl_ref[row, half:]
        return alr * xr - ali * xi + sr, alr * xi + ali * xr + si

    zero = jnp.zeros((1, half), F32)
    lax.fori_loop(0, nc, step, (zero, zero))
    y_ref[...] = jnp.dot(x_ref[...].astype(BF16), ms_ref[...], preferred_element_type=F32)
    for s in range(L):
        y_ref[:, s * LANES:] += jnp.dot(step_rows(s).astype(BF16),
                                        kc_ref[:, :(L - s) * LANES],
                                        preferred_element_type=F32)
    for t in range(L):
        y = y_ref[:, t * LANES:(t + 1) * LANES] + d_ref[...] * step_rows(t)
        o_ref[pl.ds(t, nc, stride=L), :] = _gelu(y)


def _s5(z, col_off, tables, d_skip, batch, seq, width):
    kcat, wst, mst, al = tables
    L = S5_STEP
    tiles = width // LANES
    nc = seq // L
    nstate = wst.shape[-1]
    off = col_off // LANES
    return pl.pallas_call(
        functools.partial(_s5_body, nc=nc),
        grid=(tiles, batch),
        in_specs=[
            pl.BlockSpec((seq, LANES), lambda j, b: (b, off + j)),
            pl.BlockSpec((None, LANES, L * LANES), lambda j, b: (j, 0, 0)),
            pl.BlockSpec((None, L, LANES, nstate), lambda j, b: (j, 0, 0, 0)),
            pl.BlockSpec((None, nstate, L * LANES), lambda j, b: (j, 0, 0)),
            pl.BlockSpec((None, 2, nstate // 2), lambda j, b: (j, 0, 0)),
            pl.BlockSpec((None, 1, LANES), lambda j, b: (j, 0, 0)),
        ],
        out_specs=pl.BlockSpec((seq, LANES), lambda j, b: (b, j)),
        out_shape=jax.ShapeDtypeStruct((batch * seq, width), F32),
        scratch_shapes=[pltpu.VMEM((nc, nstate), F32), pltpu.VMEM((nc, nstate), F32),
                        pltpu.VMEM((nc, L * LANES), F32)],
        compiler_params=_params(("parallel", "arbitrary")),
        name="s5",
    )(z, kcat, wst, mst, al, d_skip.astype(F32).reshape(tiles, 1, LANES))


def _ret_body(lg_ref, q_ref, k_ref, v_ref, g_ref, cos_ref, sin_ref, o_ref, r_ref, dec_ref,
              *, dk, dv, hp):
    n = RET_BLOCK
    head0 = pl.program_id(1) * hp

    @pl.when(pl.program_id(2) == 0)
    def _():
        r_ref[...] = jnp.zeros_like(r_ref)
        ti = lax.broadcasted_iota(jnp.int32, (n, n), 0)
        si = lax.broadcasted_iota(jnp.int32, (n, n), 1)
        dist = jnp.abs(ti - si).astype(F32)
        for i in range(hp):
            dec_ref[i] = jnp.where(si // CHUNK <= ti // CHUNK,
                                   jnp.exp(dist * lg_ref[head0 + i]), 0.0)

    cos = cos_ref[...]
    sin = sin_ref[...]
    hd = dk // 2
    pos = lax.broadcasted_iota(jnp.int32, (n, 1), 0).astype(F32)

    def rot(t):
        t1, t2 = t[:, :hd], t[:, hd:]
        return jnp.concatenate([t1 * cos - t2 * sin, t2 * cos + t1 * sin], axis=1)

    for i in range(hp):
        lg = lg_ref[head0 + i]
        qk_cols = slice(i * dk, (i + 1) * dk)
        v_cols = slice(i * dv, (i + 1) * dv)
        q = rot(q_ref[:, qk_cols].astype(F32))
        k = rot(k_ref[:, qk_cols].astype(F32)) * (dk ** -0.5)
        sc = lax.dot_general(q.astype(BF16), k.astype(BF16), _NT,
                             preferred_element_type=F32) * dec_ref[i]
        v = v_ref[:, v_cols]
        qd = jnp.exp((pos + 1.0) * lg)
        kd = jnp.exp((n - 1.0 - pos) * lg)
        r = r_ref[i]
        o = (jnp.dot(sc.astype(BF16), v, preferred_element_type=F32)
             + jnp.dot((q * qd).astype(BF16), r.astype(BF16), preferred_element_type=F32))
        r_ref[i] = jnp.exp(n * lg) * r + lax.dot_general((k * kd).astype(BF16), v, _TN,
                                                         preferred_element_type=F32)
        o = o * lax.rsqrt(jnp.mean(o * o, axis=-1, keepdims=True) + EPS)
        g = g_ref[:, v_cols].astype(F32)
        o_ref[:, v_cols] = (o * g * _sigmoid(g)).astype(o_ref.dtype)


def _retention(z, batch, seq, d_qk, d_v):
    dk = d_qk // C_HEADS
    dv = d_v // C_HEADS
    n = RET_BLOCK
    nblk = seq // n
    pos = jnp.arange(seq, dtype=F32)
    inv_freq = ROPE_BASE ** (-jnp.arange(0, dk, 2, dtype=F32) / dk)
    ang = pos[:, None] * inv_freq[None, :]
    log_gamma = jnp.log(1.0 - 2.0 ** (-5.0 - jnp.arange(C_HEADS, dtype=F32)))
    hp = math.gcd(RET_HEADS_PER_STEP, C_HEADS)
    hblocks = C_HEADS // hp
    wk, wv = hp * dk, hp * dv
    row = lambda b, h, c, lg: b * nblk + c
    grid_spec = pltpu.PrefetchScalarGridSpec(
        num_scalar_prefetch=1,
        grid=(batch, hblocks, nblk),
        in_specs=[
            pl.BlockSpec((n, wk), lambda b, h, c, lg: (row(b, h, c, lg), h)),
            pl.BlockSpec((n, wk), lambda b, h, c, lg: (row(b, h, c, lg), hblocks + h)),
            pl.BlockSpec((n, wv), lambda b, h, c, lg: (row(b, h, c, lg), 2 * d_qk // wv + h)),
            pl.BlockSpec((n, wv), lambda b, h, c, lg: (row(b, h, c, lg),
                                                       (2 * d_qk + d_v) // wv + h)),
            pl.BlockSpec((n, dk // 2), lambda b, h, c, lg: (c, 0)),
            pl.BlockSpec((n, dk // 2), lambda b, h, c, lg: (c, 0)),
        ],
        out_specs=pl.BlockSpec((n, wv), lambda b, h, c, lg: (row(b, h, c, lg), h)),
        scratch_shapes=[pltpu.VMEM((hp, dk, dv), F32), pltpu.VMEM((hp, n, n), F32)],
    )
    return pl.pallas_call(
        functools.partial(_ret_body, dk=dk, dv=dv, hp=hp),
        grid_spec=grid_spec,
        out_shape=jax.ShapeDtypeStruct((batch * seq, d_v), BF16),
        compiler_params=_params(("parallel", "parallel", "arbitrary")),
        name="retention",
    )(log_gamma, z, z, z, z, jnp.cos(ang), jnp.sin(ang))


def _fold_body(k_ref, w_ref, o_ref):
    o_ref[...] = lax.dot_general(k_ref[...].astype(BF16), w_ref[...].astype(BF16), _NT,
                                 preferred_element_type=F32).astype(o_ref.dtype)


def _peer_fold(sub_keys, w_q, *, td=1024):
    heads, two, nk, kh = sub_keys.shape
    d = w_q.shape[0]
    td = min(td, d)
    return pl.pallas_call(
        _fold_body,
        grid=(heads * two, d // td),
        in_specs=[pl.BlockSpec((nk, kh), lambda hp, j: (hp, 0)),
                  pl.BlockSpec((td, kh), lambda hp, j: (j, hp))],
        out_specs=pl.BlockSpec((nk, td), lambda hp, j: (hp, j)),
        out_shape=jax.ShapeDtypeStruct((heads * two * nk, d), BF16),
        compiler_params=_params(("parallel", "parallel")),
        name="peer_fold",
    )(sub_keys.reshape(heads * two * nk, kh), w_q)


def _top_k_rows(s, k):
    rows = lax.broadcasted_iota(jnp.int32, s.shape, 0).astype(F32)
    big = float(s.shape[0])
    rank = jnp.full(s.shape, float(k), F32)
    vals = []
    for i in range(k):
        m = jnp.max(s, axis=0, keepdims=True)
        first = jnp.min(jnp.where(s == m, rows, big), axis=0, keepdims=True)
        vals.append(m)
        hit = rows == first
        rank = jnp.where(hit, float(i), rank)
        s = jnp.where(hit, -jnp.inf, s)
    return jnp.concatenate(vals, axis=0), rank


def _peer_topk_body(sc_ref, e1_ref, e2_ref):
    nk = PEER_NKEYS
    k = PEER_TOPK
    s1 = sc_ref[0:nk, :]
    s2 = sc_ref[nk:2 * nk, :]
    t1, rank1 = _top_k_rows(s1, k)
    t2, rank2 = _top_k_rows(s2, k)
    sub = lax.broadcasted_iota(jnp.int32, (SUBLANES, 1), 0)
    blocks = [t1[0:1] + t2]
    for a in range(1, k):
        blocks.append(jnp.where(sub < k // (a + 1), t1[a:a + 1] + t2[0:SUBLANES], -jnp.inf))
    top, crank = _top_k_rows(jnp.concatenate(blocks, axis=0), k)
    chosen = (crank < float(k)).astype(F32)
    cnt_a = [jnp.sum(chosen[0:k], axis=0, keepdims=True)]
    for a in range(1, k):
        lo = k + (a - 1) * SUBLANES
        cnt_a.append(jnp.sum(chosen[lo:lo + SUBLANES], axis=0, keepdims=True))
    cnt = jnp.zeros(s1.shape, F32)
    for a in range(k):
        cnt = cnt + jnp.where(rank1 == float(a), cnt_a[a], 0.0)
    z = jnp.sum(jnp.exp(top - top[0:1]), axis=0, keepdims=True)
    e1_ref[0:nk, :] = jnp.exp(s1 - t1[0:1]) / z
    e1_ref[nk:2 * nk, :] = cnt
    e2_ref[0:nk, :] = jnp.exp(s2 - t2[0:1])
    e2_ref[nk:2 * nk, :] = rank2


def _peer_topk(sc_t, heads, *, tt=256):
    rows, t = sc_t.shape
    per = rows // heads
    tt = math.gcd(tt, t)
    spec = pl.BlockSpec((per, tt), lambda i, h: (h, i))
    return pl.pallas_call(
        _peer_topk_body,
        grid=(t // tt, heads),
        in_specs=[spec],
        out_specs=[spec, spec],
        out_shape=[jax.ShapeDtypeStruct((rows, t), F32)] * 2,
        compiler_params=_params(("parallel", "parallel")),
        name="peer_topk",
    )(sc_t)


def _peer_main_body(xn_ref, u_ref, v_ref, e1_ref, e2_ref, o_ref, p_ref, w_ref, *, heads):
    j = pl.program_id(1)

    @pl.when(j == 0)
    def _():
        o_ref[...] = jnp.zeros_like(o_ref)

    nk = PEER_NKEYS
    te, tt = p_ref.shape
    n_e1 = te // nk
    assert SUBLANES % n_e1 == 0
    first = j * n_e1
    off = first % SUBLANES

    def key_rows(base):
        start = pl.multiple_of(base + (first // SUBLANES) * SUBLANES, SUBLANES)
        grp = e1_ref[pl.ds(start, SUBLANES), :]
        out = []
        for r in range(n_e1):
            row = grp[r:r + 1]
            for o in range(n_e1, SUBLANES, n_e1):
                row = jnp.where(off == o, grp[o + r:o + r + 1], row)
            out.append(row)
        return out

    a1_rows = [key_rows(h * 2 * nk) for h in range(heads)]
    cnt_rows = [key_rows(h * 2 * nk + nk) for h in range(heads)]
    for r in range(n_e1):
        rows = slice(r * nk, (r + 1) * nk)
        a1 = [a1_rows[h][r] for h in range(heads)]
        cnt = [cnt_rows[h][r] for h in range(heads)]
        for c in range(tt // LANES):
            lanes = slice(c * LANES, (c + 1) * LANES)
            w = jnp.zeros((nk, LANES), F32)
            for h in range(heads):
                base = h * 2 * nk
                a2 = e2_ref[base:base + nk, lanes]
                rank2 = e2_ref[base + nk:base + 2 * nk, lanes]
                w = w + jnp.where(rank2 < cnt[h][:, lanes], a2 * a1[h][:, lanes], 0.0)
            w_ref[rows, lanes] = w
    act = _gelu(lax.dot_general(u_ref[...], xn_ref[...], _NT, preferred_element_type=F32))
    p_ref[...] = (act * w_ref[...]).astype(p_ref.dtype)
    o_ref[...] += lax.dot_general(p_ref[...], v_ref[...], _TN, preferred_element_type=F32)


def _peer_main(xn, u16, v16, layer, e1_t, e2_t, heads, *, tt=512, te=512):
    t, d = xn.shape
    n_exp = PEER_NKEYS * PEER_NKEYS
    tt = min(tt, t)
    rows = e1_t.shape[0]
    tile0 = layer * (n_exp // te)
    once = dict(pipeline_mode=pl.Buffered(1))
    return pl.pallas_call(
        functools.partial(_peer_main_body, heads=heads),
        grid=(t // tt, n_exp // te),
        in_specs=[
            pl.BlockSpec((tt, d), lambda i, j: (i, 0), **once),
            pl.BlockSpec((te, d), lambda i, j: (tile0 + j, 0)),
            pl.BlockSpec((te, d), lambda i, j: (tile0 + j, 0)),
            pl.BlockSpec((rows, tt), lambda i, j: (0, i), **once),
            pl.BlockSpec((rows, tt), lambda i, j: (0, i), **once),
        ],
        out_specs=pl.BlockSpec((tt, d), lambda i, j: (i, 0)),
        out_shape=jax.ShapeDtypeStruct((t, d), F32),
        scratch_shapes=[pltpu.VMEM((te, tt), BF16), pltpu.VMEM((te, tt), F32)],
        compiler_params=_params(("parallel", "arbitrary")),
        name="peer_main",
    )(xn, u16, v16, e1_t, e2_t)


def _peer(xn, w_q, sub_keys, u16, v16, layer):
    heads = sub_keys.shape[0]
    fold = _peer_fold(sub_keys, w_q)
    sc_t = _matmul(fold, xn, trans_b=True, name="peer_scores")
    e1_t, e2_t = _peer_topk(sc_t, heads)
    return _peer_main(xn, u16, v16, layer, e1_t, e2_t, heads)


def kernel(x, mix_norm_g, ab_w_in, a_lb_param, b_a_re, b_a_im, b_log_dt, b_b_re, b_b_im, b_c_re, b_c_im, b_d, b_w_glu, ab_w_out, c_w_in, c_w_out, ffn_norm_g, peer_w_q, peer_sub_keys, peer_u, peer_v, final_norm_g):
    batch, seq, d = x.shape
    depth = mix_norm_g.shape[0]
    xs = x.reshape(batch * seq, d)
    u16 = peer_u.astype(BF16).reshape(-1, d)
    v16 = peer_v.astype(BF16).reshape(-1, d)
    delta = None
    for layer in range(depth):
        j = layer // 2
        if delta is None:
            h = _rmsnorm(xs, mix_norm_g[layer])
        else:
            xs, h = _rmsnorm(xs, mix_norm_g[layer], delta, emit_x=True)
        if layer % 2 == 0:
            a_width = a_lb_param.shape[1]
            b_width = b_w_glu.shape[1]
            z = _matmul(h, ab_w_in[j].astype(BF16), name="ab_in")
            o = _hgrn2(z, a_lb_param.astype(F32), j, batch, seq, a_width, a_width + b_width)
            tables = _s5_tables(b_a_re[j], b_a_im[j], b_log_dt[j], b_b_re[j], b_b_im[j],
                                b_c_re[j], b_c_im[j])
            yg = _s5(z, 4 * a_width, tables, b_d[j], batch, seq, b_width)
            o = _matmul(yg, b_w_glu[j].astype(BF16), extra=yg, epilogue="glu",
                        into=(o, a_width), name="s5_glu")
            xs = _matmul(o, ab_w_out[j].astype(BF16), extra=xs, epilogue="residual",
                         tm=512, name="ab_out")
        else:
            d_v = c_w_out.shape[1]
            d_qk = (c_w_in.shape[2] - 2 * d_v) // 2
            z = _matmul(h, c_w_in[j].astype(BF16), out_dtype=BF16, name="ret_in")
            o = _retention(z, batch, seq, d_qk, d_v)
            xs = _matmul(o, c_w_out[j].astype(BF16), extra=xs, epilogue="residual",
                         tm=512, tn=512, tk=8192, name="ret_out")
        hn = _rmsnorm(xs, ffn_norm_g[layer])
        delta = _peer(hn, peer_w_q[layer], peer_sub_keys[layer], u16, v16, layer)
    out = _rmsnorm(xs, final_norm_g, delta, out_dtype=F32)
    return out.reshape(batch, seq, d)
```

```python
import functools
import math

import jax
import jax.numpy as jnp
from jax import lax
from jax.experimental import pallas as pl
from jax.experimental.pallas import tpu as pltpu

EPS = 1e-6
CHUNK = 64
A_KEY = 128
B_GROUP = 16
B_STATE = 64
S5_DT_FLOOR = -1e-4
S5_STEP = 16
C_HEADS = 16
ROPE_BASE = 10000.0
RET_BLOCK = 4 * CHUNK
RET_HEADS_PER_STEP = 2
PEER_TOPK = 16
PEER_NKEYS = 128

LANES = 128
SUBLANES = 8
V7X_VMEM_LIMIT = 56 * 1024 * 1024

F32 = jnp.float32
BF16 = jnp.bfloat16

_NN = (((1,), (0,)), ((), ()))
_NT = (((1,), (1,)), ((), ()))
_TN = (((0,), (0,)), ((), ()))


def _params(sem):
    return pltpu.CompilerParams(dimension_semantics=sem, vmem_limit_bytes=V7X_VMEM_LIMIT)


def _sigmoid(x):
    return 1.0 / (1.0 + jnp.exp(-x))


def _gelu(x):
    return 0.5 * x * (1.0 + lax.erf(x * math.sqrt(0.5)))


def _norm_body(x_ref, g_ref, h_ref):
    x = x_ref[...]
    ms = jnp.mean(x * x, axis=-1, keepdims=True)
    h_ref[...] = (x * lax.rsqrt(ms + EPS) * g_ref[...]).astype(h_ref.dtype)


def _rmsnorm(x, g, *, out_dtype=BF16, tm=512):
    m, d = x.shape
    tm = math.gcd(tm, m)
    row = pl.BlockSpec((tm, d), lambda i: (i, 0))
    return pl.pallas_call(
        _norm_body,
        grid=(m // tm,),
        in_specs=[row, pl.BlockSpec((1, d), lambda i: (0, 0))],
        out_specs=row,
        out_shape=jax.ShapeDtypeStruct((m, d), out_dtype),
        compiler_params=_params(("parallel",)),
        name="rmsnorm",
    )(x, g.reshape(1, d))


def _mm_body(*refs, nk, trans_b, epilogue, aliased):
    a_ref, b_ref = refs[0], refs[1]
    e_ref = refs[2] if epilogue else None
    n_in = 2 + bool(epilogue) + aliased
    o_ref = refs[n_in]
    acc_ref = refs[n_in + 1] if nk > 1 else None
    part = lax.dot_general(a_ref[...].astype(b_ref.dtype), b_ref[...], _NT if trans_b else _NN,
                           preferred_element_type=F32)

    def finish(acc):
        if epilogue == "residual":
            acc = acc + e_ref[...]
        elif epilogue == "glu":
            acc = e_ref[...].astype(F32) * _sigmoid(acc)
        o_ref[...] = acc.astype(o_ref.dtype)

    if nk == 1:
        finish(part)
        return
    k = pl.program_id(2)

    @pl.when(k == 0)
    def _():
        acc_ref[...] = part

    @pl.when(k > 0)
    def _():
        acc_ref[...] += part

    @pl.when(k == nk - 1)
    def _():
        finish(acc_ref[...])


def _matmul(a, b, *, trans_b=False, extra=None, epilogue=None, out_dtype=F32, into=None,
            tm=1024, tn=1024, tk=4096, name="matmul"):
    m, kdim = a.shape
    n = b.shape[0] if trans_b else b.shape[1]
    tm, tn, tk = math.gcd(tm, m), math.gcd(tn, n), math.gcd(tk, kdim)
    if into is not None:
        tn = math.gcd(tn, into[1])
    nk = kdim // tk
    a_spec = pl.BlockSpec((tm, tk), lambda i, j, k: (i, k))
    if trans_b:
        b_spec = pl.BlockSpec((tn, tk), lambda i, j, k: (j, k))
    else:
        b_spec = pl.BlockSpec((tk, tn), lambda i, j, k: (k, j))
    e_spec = pl.BlockSpec((tm, tn), lambda i, j, k: (i, j))
    in_specs = [a_spec, b_spec] + ([e_spec] if epilogue else [])
    args = [a, b] + ([extra] if epilogue else [])
    if into is None:
        o_spec, out_shape, aliases = e_spec, jax.ShapeDtypeStruct((m, n), out_dtype), {}
    else:
        dest, col = into
        joff = col // tn
        o_spec = pl.BlockSpec((tm, tn), lambda i, j, k: (i, j + joff))
        out_shape = jax.ShapeDtypeStruct(dest.shape, dest.dtype)
        aliases = {len(args): 0}
        in_specs = in_specs + [pl.BlockSpec(memory_space=pl.ANY)]
        args = args + [dest]
    scratch = [pltpu.VMEM((tm, tn), F32)] if nk > 1 else []
    return pl.pallas_call(
        functools.partial(_mm_body, nk=nk, trans_b=trans_b, epilogue=epilogue,
                          aliased=into is not None),
        grid=(m // tm, n // tn, nk),
        in_specs=in_specs,
        out_specs=o_spec,
        out_shape=out_shape,
        input_output_aliases=aliases,
        scratch_shapes=scratch,
        compiler_params=_params(("parallel", "parallel", "arbitrary")),
        name=name,
    )(*args)


HG_SUB = 16


def _hgrn2_body(lbp_ref, q_ref, f_ref, i_ref, g_ref, o_ref, st_ref, *, n_chunks, slot, hpb):
    @pl.when(pl.program_id(2) == 0)
    def _():
        st_ref[...] = jnp.zeros_like(st_ref)

    p = lbp_ref[...]
    ex = jnp.exp(p - jnp.max(p, axis=0, keepdims=True))
    sm = ex / jnp.sum(ex, axis=0, keepdims=True)
    lb = jnp.sum(sm[: slot + 1], axis=0, keepdims=True)

    ri = lax.broadcasted_iota(jnp.int32, (CHUNK, CHUNK), 0)
    ci = lax.broadcasted_iota(jnp.int32, (CHUNK, CHUNK), 1)
    tri = (ci <= ri).astype(F32)
    sub_row = lax.broadcasted_iota(jnp.int32, (HG_SUB, 1), 0)
    n_sub = CHUNK // HG_SUB
    heads_cols = [slice(hh * A_KEY, (hh + 1) * A_KEY) for hh in range(hpb)]

    def per_head(fn):
        return jnp.concatenate([fn(hh, cols) for hh, cols in enumerate(heads_cols)], axis=1)

    def head_sum(x):
        return per_head(lambda hh, cols: jnp.broadcast_to(
            jnp.sum(x[:, cols], axis=-1, keepdims=True), (x.shape[0], A_KEY)))

    def chunk(ci_, carry):
        rows = pl.ds(pl.multiple_of(ci_ * CHUNK, CHUNK), CHUNK)
        f = lb + (1.0 - lb) * _sigmoid(f_ref[rows, :])
        qr = q_ref[rows, :]
        q = qr * _sigmoid(qr)
        k = 1.0 - f
        v = i_ref[rows, :]
        cum = jnp.dot(tri, jnp.log(f), precision=lax.Precision.HIGHEST,
                      preferred_element_type=F32)
        v16 = v.astype(BF16)
        parts = []
        for sb in range(n_sub):
            lo = sb * HG_SUB
            cs = cum[lo:lo + HG_SUB]
            qs = q[lo:lo + HG_SUB]
            ks = k[lo:lo + HG_SUB]
            vs = v[lo:lo + HG_SUB]
            acc = jnp.zeros(cs.shape, F32)
            for s in range(HG_SUB):
                diff = jnp.where(sub_row >= s, cs - cs[s:s + 1], -jnp.inf)
                acc = acc + head_sum(qs * jnp.exp(diff) * ks[s:s + 1]) * vs[s:s + 1]
            if sb > 0:
                ref = cum[lo - 1:lo]
                qx = (qs * jnp.exp(cs - ref)).astype(BF16)
                kx = (k[:lo] * jnp.exp(ref - cum[:lo])).astype(BF16)

                def off_diag(hh, cols):
                    att = lax.dot_general(qx[:, cols], kx[:, cols], _NT,
                                          preferred_element_type=F32)
                    return jnp.dot(att.astype(BF16), v16[:lo, cols], preferred_element_type=F32)

                acc = acc + per_head(off_diag)
            parts.append(acc)
        o = jnp.concatenate(parts, axis=0)
        last = cum[CHUNK - 1:CHUNK]
        qd = (q * jnp.exp(cum)).astype(BF16)
        kd = (k * jnp.exp(last - cum)).astype(BF16)
        decay = jnp.exp(last)
        states = [st_ref[hh] for hh in range(hpb)]
        o = o + per_head(lambda hh, cols: lax.dot_general(
            qd[:, cols], states[hh].astype(BF16), _NT, preferred_element_type=F32))
        for hh, cols in enumerate(heads_cols):
            st_ref[hh] = states[hh] * decay[:, cols] + lax.dot_general(
                v16[:, cols], kd[:, cols], _TN, preferred_element_type=F32)
        o = o * lax.rsqrt(head_sum(o * o) * (1.0 / A_KEY) + EPS)
        gr = g_ref[rows, :]
        o_ref[rows, :] = (o * gr * _sigmoid(gr)).astype(o_ref.dtype)
        return carry

    lax.fori_loop(0, n_chunks, chunk, 0)


def _hgrn2(z, lb_param, slot, batch, seq, width, out_width, *, rows=512, heads_per_block=8):
    heads = width // A_KEY
    hpb = math.gcd(heads_per_block, heads)
    hblocks = heads // hpb
    rows = math.gcd(rows, seq)
    nblk = seq // rows
    lane = lambda off: pl.BlockSpec(
        (rows, hpb * A_KEY), lambda b, h, c, off=off: (b * nblk + c, off * hblocks + h))
    n_slots = lb_param.shape[0]
    return pl.pallas_call(
        functools.partial(_hgrn2_body, n_chunks=rows // CHUNK, slot=slot, hpb=hpb),
        grid=(batch, hblocks, nblk),
        in_specs=[pl.BlockSpec((n_slots, hpb * A_KEY), lambda b, h, c: (0, h)),
                  lane(0), lane(1), lane(2), lane(3)],
        out_specs=pl.BlockSpec((rows, hpb * A_KEY), lambda b, h, c: (b * nblk + c, h)),
        out_shape=jax.ShapeDtypeStruct((batch * seq, out_width), BF16),
        scratch_shapes=[pltpu.VMEM((hpb, A_KEY, A_KEY), F32)],
        compiler_params=_params(("parallel", "parallel", "arbitrary")),
        name="hgrn2",
    )(lb_param, z, z, z, z)


def _s5_tables(a_re, a_im, log_dt, b_re, b_im, c_re, c_im):
    hi = lax.Precision.HIGHEST
    L = S5_STEP
    groups, states = a_re.shape
    tiles = groups * B_GROUP // LANES
    gpt = LANES // B_GROUP
    dt = jnp.exp(log_dt.astype(F32))[:, None]
    lam_re = jnp.minimum(a_re.astype(F32), S5_DT_FLOOR)
    lam_im = a_im.astype(F32)
    mag = jnp.exp(lam_re * dt)
    ang = lam_im * dt
    ab_re = mag * jnp.cos(ang)
    ab_im = mag * jnp.sin(ang)
    nr = ab_re - 1.0
    ni = ab_im
    den = lam_re * lam_re + lam_im * lam_im
    z_re = (nr * lam_re + ni * lam_im) / den
    z_im = (ni * lam_re - nr * lam_im) / den
    br = b_re.astype(F32)
    bi = b_im.astype(F32)
    bb_re = z_re[..., None] * br - z_im[..., None] * bi
    bb_im = z_re[..., None] * bi + z_im[..., None] * br
    pw_re, pw_im = [jnp.ones_like(ab_re)], [jnp.zeros_like(ab_re)]
    for _ in range(L):
        pr, pi = pw_re[-1], pw_im[-1]
        pw_re.append(pr * ab_re - pi * ab_im)
        pw_im.append(pr * ab_im + pi * ab_re)
    pw_re = jnp.stack(pw_re)
    pw_im = jnp.stack(pw_im)
    cr = c_re.astype(F32)[None]
    ci = c_im.astype(F32)[None]
    cp_re = cr * pw_re[:, :, None, :] - ci * pw_im[:, :, None, :]
    cp_im = cr * pw_im[:, :, None, :] + ci * pw_re[:, :, None, :]
    lag = (jnp.einsum("lghp,gpk->lghk", cp_re[:L], bb_re, precision=hi)
           - jnp.einsum("lghp,gpk->lghk", cp_im[:L], bb_im, precision=hi))
    def embed(compact, inner, row_group):
        cols = compact.shape[-1]
        wide_col = jnp.arange(cols * gpt)
        rep = jnp.arange(cols)[:, None] == (wide_col // (gpt * inner)) * inner + wide_col % inner
        wide = jnp.matmul(compact, rep.astype(BF16), preferred_element_type=F32)
        keep = row_group[:, None] == ((wide_col // inner) % gpt)[None, :]
        return jnp.where(keep, wide, 0.0).astype(BF16)

    key_group = jnp.arange(LANES) // B_GROUP
    state_group = (jnp.arange(2 * gpt * states) // states) % gpt
    kc = lag.reshape(L, tiles, gpt, B_GROUP, B_GROUP).astype(BF16)
    kc = kc.transpose(1, 2, 4, 0, 3).reshape(tiles, LANES, L * B_GROUP)
    kcat = embed(kc, B_GROUP, key_group)
    rev = slice(L - 1, None, -1)
    wb_re = pw_re[rev][..., None] * bb_re[None] - pw_im[rev][..., None] * bb_im[None]
    wb_im = pw_re[rev][..., None] * bb_im[None] + pw_im[rev][..., None] * bb_re[None]
    wb = jnp.stack([wb_re, wb_im], axis=1).astype(BF16)
    wb = wb.reshape(L, 2, tiles, gpt, states, B_GROUP).transpose(2, 0, 3, 5, 1, 4)
    wst = embed(wb.reshape(tiles, L, LANES, 2 * states), states, key_group)
    mo = jnp.stack([cp_re[1:], -cp_im[1:]], axis=1).astype(BF16)
    mo = mo.reshape(L, 2, tiles, gpt, B_GROUP, states).transpose(2, 1, 3, 5, 0, 4)
    mst = embed(mo.reshape(tiles, 2 * gpt * states, L * B_GROUP), B_GROUP, state_group)
    al = jnp.stack([pw_re[L].reshape(tiles, gpt * states),
                    pw_im[L].reshape(tiles, gpt * states)], axis=1)
    return kcat, wst, mst, al


def _s5_body(u_ref, kc_ref, ws_ref, ms_ref, al_ref, d_ref, o_ref, sl_ref, x_ref, y_ref, *, nc):
    L = S5_STEP
    half = sl_ref.shape[1] // 2
    step_rows = lambda s: u_ref[pl.ds(s, nc, stride=L), :]
    frames = jnp.concatenate([step_rows(s).astype(BF16) for s in range(L)], axis=1)
    sl_ref[...] = jnp.dot(frames, ws_ref[...].reshape(L * LANES, 2 * half),
                          preferred_element_type=F32)
    alr = al_ref[0:1, :]
    ali = al_ref[1:2, :]

    def step(c, carry):
        xr, xi = carry
        row = pl.ds(c, 1)
        x_ref[row, 0:half] = xr
        x_ref[row, half:] = xi
        sr = sl_ref[row, 0:half]
        si = sl_ref[row, half:]
        return alr * xr - ali * xi + sr, alr * xi + ali * xr + si

    zero = jnp.zeros((1, half), F32)
    lax.fori_loop(0, nc, step, (zero, zero))
    y_ref[...] = jnp.dot(x_ref[...].astype(BF16), ms_ref[...], preferred_element_type=F32)
    for s in range(L):
        y_ref[:, s * LANES:] += jnp.dot(step_rows(s).astype(BF16),
                                        kc_ref[:, :(L - s) * LANES],
                                        preferred_element_type=F32)
    for t in range(L):
        y = y_ref[:, t * LANES:(t + 1) * LANES] + d_ref[...] * step_rows(t)
        o_ref[pl.ds(t, nc, stride=L), :] = _gelu(y)


def _s5(z, col_off, tables, d_skip, batch, seq, width):
    kcat, wst, mst, al = tables
    L = S5_STEP
    tiles = width // LANES
    nc = seq // L
    nstate = wst.shape[-1]
    off = col_off // LANES
    return pl.pallas_call(
        functools.partial(_s5_body, nc=nc),
        grid=(tiles, batch),
        in_specs=[
            pl.BlockSpec((seq, LANES), lambda j, b: (b, off + j)),
            pl.BlockSpec((None, LANES, L * LANES), lambda j, b: (j, 0, 0)),
            pl.BlockSpec((None, L, LANES, nstate), lambda j, b: (j, 0, 0, 0)),
            pl.BlockSpec((None, nstate, L * LANES), lambda j, b: (j, 0, 0)),
            pl.BlockSpec((None, 2, nstate // 2), lambda j, b: (j, 0, 0)),
            pl.BlockSpec((None, 1, LANES), lambda j, b: (j, 0, 0)),
        ],
        out_specs=pl.BlockSpec((seq, LANES), lambda j, b: (b, j)),
        out_shape=jax.ShapeDtypeStruct((batch * seq, width), F32),
        scratch_shapes=[pltpu.VMEM((nc, nstate), F32), pltpu.VMEM((nc, nstate), F32),
                        pltpu.VMEM((nc, L * LANES), F32)],
        compiler_params=_params(("parallel", "arbitrary")),
        name="s5",
    )(z, kcat, wst, mst, al, d_skip.astype(F32).reshape(tiles, 1, LANES))


def _ret_body(lg_ref, q_ref, k_ref, v_ref, g_ref, cos_ref, sin_ref, o_ref, r_ref, dec_ref,
              *, dk, dv, hp):
    n = RET_BLOCK
    head0 = pl.program_id(1) * hp

    @pl.when(pl.program_id(2) == 0)
    def _():
        r_ref[...] = jnp.zeros_like(r_ref)
        ti = lax.broadcasted_iota(jnp.int32, (n, n), 0)
        si = lax.broadcasted_iota(jnp.int32, (n, n), 1)
        dist = jnp.abs(ti - si).astype(F32)
        for i in range(hp):
            dec_ref[i] = jnp.where(si // CHUNK <= ti // CHUNK,
                                   jnp.exp(dist * lg_ref[head0 + i]), 0.0)

    cos = cos_ref[...]
    sin = sin_ref[...]
    hd = dk // 2
    pos = lax.broadcasted_iota(jnp.int32, (n, 1), 0).astype(F32)

    def rot(t):
        t1, t2 = t[:, :hd], t[:, hd:]
        return jnp.concatenate([t1 * cos - t2 * sin, t2 * cos + t1 * sin], axis=1)

    for i in range(hp):
        lg = lg_ref[head0 + i]
        qk_cols = slice(i * dk, (i + 1) * dk)
        v_cols = slice(i * dv, (i + 1) * dv)
        q = rot(q_ref[:, qk_cols].astype(F32))
        k = rot(k_ref[:, qk_cols].astype(F32)) * (dk ** -0.5)
        sc = lax.dot_general(q.astype(BF16), k.astype(BF16), _NT,
                             preferred_element_type=F32) * dec_ref[i]
        v = v_ref[:, v_cols]
        qd = jnp.exp((pos + 1.0) * lg)
        kd = jnp.exp((n - 1.0 - pos) * lg)
        r = r_ref[i]
        o = (jnp.dot(sc.astype(BF16), v, preferred_element_type=F32)
             + jnp.dot((q * qd).astype(BF16), r.astype(BF16), preferred_element_type=F32))
        r_ref[i] = jnp.exp(n * lg) * r + lax.dot_general((k * kd).astype(BF16), v, _TN,
                                                         preferred_element_type=F32)
        o = o * lax.rsqrt(jnp.mean(o * o, axis=-1, keepdims=True) + EPS)
        g = g_ref[:, v_cols].astype(F32)
        o_ref[:, v_cols] = (o * g * _sigmoid(g)).astype(o_ref.dtype)


def _retention(z, batch, seq, d_qk, d_v):
    dk = d_qk // C_HEADS
    dv = d_v // C_HEADS
    n = RET_BLOCK
    nblk = seq // n
    pos = jnp.arange(seq, dtype=F32)
    inv_freq = ROPE_BASE ** (-jnp.arange(0, dk, 2, dtype=F32) / dk)
    ang = pos[:, None] * inv_freq[None, :]
    log_gamma = jnp.log(1.0 - 2.0 ** (-5.0 - jnp.arange(C_HEADS, dtype=F32)))
    hp = math.gcd(RET_HEADS_PER_STEP, C_HEADS)
    hblocks = C_HEADS // hp
    wk, wv = hp * dk, hp * dv
    row = lambda b, h, c, lg: b * nblk + c
    grid_spec = pltpu.PrefetchScalarGridSpec(
        num_scalar_prefetch=1,
        grid=(batch, hblocks, nblk),
        in_specs=[
            pl.BlockSpec((n, wk), lambda b, h, c, lg: (row(b, h, c, lg), h)),
            pl.BlockSpec((n, wk), lambda b, h, c, lg: (row(b, h, c, lg), hblocks + h)),
            pl.BlockSpec((n, wv), lambda b, h, c, lg: (row(b, h, c, lg), 2 * d_qk // wv + h)),
            pl.BlockSpec((n, wv), lambda b, h, c, lg: (row(b, h, c, lg),
                                                       (2 * d_qk + d_v) // wv + h)),
            pl.BlockSpec((n, dk // 2), lambda b, h, c, lg: (c, 0)),
            pl.BlockSpec((n, dk // 2), lambda b, h, c, lg: (c, 0)),
        ],
        out_specs=pl.BlockSpec((n, wv), lambda b, h, c, lg: (row(b, h, c, lg), h)),
        scratch_shapes=[pltpu.VMEM((hp, dk, dv), F32), pltpu.VMEM((hp, n, n), F32)],
    )
    return pl.pallas_call(
        functools.partial(_ret_body, dk=dk, dv=dv, hp=hp),
        grid_spec=grid_spec,
        out_shape=jax.ShapeDtypeStruct((batch * seq, d_v), BF16),
        compiler_params=_params(("parallel", "parallel", "arbitrary")),
        name="retention",
    )(log_gamma, z, z, z, z, jnp.cos(ang), jnp.sin(ang))


def _fold_body(k_ref, w_ref, o_ref):
    o_ref[...] = lax.dot_general(k_ref[...].astype(BF16), w_ref[...].astype(BF16), _NT,
                                 preferred_element_type=F32).astype(o_ref.dtype)


def _peer_fold(sub_keys, w_q, *, td=1024):
    heads, two, nk, kh = sub_keys.shape
    d = w_q.shape[0]
    td = min(td, d)
    return pl.pallas_call(
        _fold_body,
        grid=(heads * two, d // td),
        in_specs=[pl.BlockSpec((nk, kh), lambda hp, j: (hp, 0)),
                  pl.BlockSpec((td, kh), lambda hp, j: (j, hp))],
        out_specs=pl.BlockSpec((nk, td), lambda hp, j: (hp, j)),
        out_shape=jax.ShapeDtypeStruct((heads * two * nk, d), BF16),
        compiler_params=_params(("parallel", "parallel")),
        name="peer_fold",
    )(sub_keys.reshape(heads * two * nk, kh), w_q)


def _top_k_rows(s, k):
    rows = lax.broadcasted_iota(jnp.int32, s.shape, 0).astype(F32)
    big = float(s.shape[0])
    rank = jnp.full(s.shape, float(k), F32)
    vals = []
    for i in range(k):
        m = jnp.max(s, axis=0, keepdims=True)
        first = jnp.min(jnp.where(s == m, rows, big), axis=0, keepdims=True)
        vals.append(m)
        hit = rows == first
        rank = jnp.where(hit, float(i), rank)
        s = jnp.where(hit, -jnp.inf, s)
    return jnp.concatenate(vals, axis=0), rank


def _peer_topk_body(sc_ref, e1_ref, e2_ref):
    nk = PEER_NKEYS
    k = PEER_TOPK
    s1 = sc_ref[0:nk, :]
    s2 = sc_ref[nk:2 * nk, :]
    t1, rank1 = _top_k_rows(s1, k)
    t2, rank2 = _top_k_rows(s2, k)
    sub = lax.broadcasted_iota(jnp.int32, (SUBLANES, 1), 0)
    blocks = [t1[0:1] + t2]
    for a in range(1, k):
        blocks.append(jnp.where(sub < k // (a + 1), t1[a:a + 1] + t2[0:SUBLANES], -jnp.inf))
    top, crank = _top_k_rows(jnp.concatenate(blocks, axis=0), k)
    chosen = (crank < float(k)).astype(F32)
    cnt_a = [jnp.sum(chosen[0:k], axis=0, keepdims=True)]
    for a in range(1, k):
        lo = k + (a - 1) * SUBLANES
        cnt_a.append(jnp.sum(chosen[lo:lo + SUBLANES], axis=0, keepdims=True))
    cnt = jnp.zeros(s1.shape, F32)
    for a in range(k):
        cnt = cnt + jnp.where(rank1 == float(a), cnt_a[a], 0.0)
    z = jnp.sum(jnp.exp(top - top[0:1]), axis=0, keepdims=True)
    e1_ref[0:nk, :] = jnp.exp(s1 - t1[0:1]) / z
    e1_ref[nk:2 * nk, :] = cnt
    e2_ref[0:nk, :] = jnp.exp(s2 - t2[0:1])
    e2_ref[nk:2 * nk, :] = rank2


def _peer_topk(sc_t, heads, *, tt=256):
    rows, t = sc_t.shape
    per = rows // heads
    tt = math.gcd(tt, t)
    spec = pl.BlockSpec((per, tt), lambda i, h: (h, i))
    return pl.pallas_call(
        _peer_topk_body,
        grid=(t // tt, heads),
        in_specs=[spec],
        out_specs=[spec, spec],
        out_shape=[jax.ShapeDtypeStruct((rows, t), F32)] * 2,
        compiler_params=_params(("parallel", "parallel")),
        name="peer_topk",
    )(sc_t)


def _peer_main_body(xn_ref, u_ref, v_ref, e1_ref, e2_ref, r_ref, o_ref, p_ref, w_ref, *, heads):
    j = pl.program_id(1)

    @pl.when(j == 0)
    def _():
        o_ref[...] = r_ref[...]

    nk = PEER_NKEYS
    te, tt = p_ref.shape
    n_e1 = te // nk
    assert SUBLANES % n_e1 == 0
    first = j * n_e1
    off = first % SUBLANES

    def key_rows(base):
        start = pl.multiple_of(base + (first // SUBLANES) * SUBLANES, SUBLANES)
        grp = e1_ref[pl.ds(start, SUBLANES), :]
        out = []
        for r in range(n_e1):
            row = grp[r:r + 1]
            for o in range(n_e1, SUBLANES, n_e1):
                row = jnp.where(off == o, grp[o + r:o + r + 1], row)
            out.append(row)
        return out

    a1_rows = [key_rows(h * 2 * nk) for h in range(heads)]
    cnt_rows = [key_rows(h * 2 * nk + nk) for h in range(heads)]
    for r in range(n_e1):
        rows = slice(r * nk, (r + 1) * nk)
        a1 = [a1_rows[h][r] for h in range(heads)]
        cnt = [cnt_rows[h][r] for h in range(heads)]
        for c in range(tt // LANES):
            lanes = slice(c * LANES, (c + 1) * LANES)
            w = jnp.zeros((nk, LANES), F32)
            for h in range(heads):
                base = h * 2 * nk
                a2 = e2_ref[base:base + nk, lanes]
                rank2 = e2_ref[base + nk:base + 2 * nk, lanes]
                w = w + jnp.where(rank2 < cnt[h][:, lanes], a2 * a1[h][:, lanes], 0.0)
            w_ref[rows, lanes] = w
    act = _gelu(lax.dot_general(u_ref[...], xn_ref[...], _NT, preferred_element_type=F32))
    p_ref[...] = (act * w_ref[...]).astype(p_ref.dtype)
    o_ref[...] += lax.dot_general(p_ref[...], v_ref[...], _TN, preferred_element_type=F32)


def _peer_main(xn, u16, v16, layer, e1_t, e2_t, resid, heads, *, tt=512, te=512):
    t, d = xn.shape
    n_exp = PEER_NKEYS * PEER_NKEYS
    tt = min(tt, t)
    rows = e1_t.shape[0]
    tile0 = layer * (n_exp // te)
    once = dict(pipeline_mode=pl.Buffered(1))
    return pl.pallas_call(
        functools.partial(_peer_main_body, heads=heads),
        grid=(t // tt, n_exp // te),
        in_specs=[
            pl.BlockSpec((tt, d), lambda i, j: (i, 0), **once),
            pl.BlockSpec((te, d), lambda i, j: (tile0 + j, 0)),
            pl.BlockSpec((te, d), lambda i, j: (tile0 + j, 0)),
            pl.BlockSpec((rows, tt), lambda i, j: (0, i), **once),
            pl.BlockSpec((rows, tt), lambda i, j: (0, i), **once),
            pl.BlockSpec((tt, d), lambda i, j: (i, 0), **once),
        ],
        out_specs=pl.BlockSpec((tt, d), lambda i, j: (i, 0)),
        out_shape=jax.ShapeDtypeStruct((t, d), F32),
        scratch_shapes=[pltpu.VMEM((te, tt), BF16), pltpu.VMEM((te, tt), F32)],
        compiler_params=_params(("parallel", "arbitrary")),
        name="peer_main",
    )(xn, u16, v16, e1_t, e2_t, resid)


def _peer(xs, xn, w_q, sub_keys, u16, v16, layer):
    heads = sub_keys.shape[0]
    fold = _peer_fold(sub_keys, w_q)
    sc_t = _matmul(fold, xn, trans_b=True, name="peer_scores")
    e1_t, e2_t = _peer_topk(sc_t, heads)
    return _peer_main(xn, u16, v16, layer, e1_t, e2_t, xs, heads)


def kernel(x, mix_norm_g, ab_w_in, a_lb_param, b_a_re, b_a_im, b_log_dt, b_b_re, b_b_im, b_c_re, b_c_im, b_d, b_w_glu, ab_w_out, c_w_in, c_w_out, ffn_norm_g, peer_w_q, peer_sub_keys, peer_u, peer_v, final_norm_g):
    batch, seq, d = x.shape
    depth = mix_norm_g.shape[0]
    xs = x.reshape(batch * seq, d)
    u16 = peer_u.astype(BF16).reshape(-1, d)
    v16 = peer_v.astype(BF16).reshape(-1, d)
    for layer in range(depth):
        j = layer // 2
        h = _rmsnorm(xs, mix_norm_g[layer])
        if layer % 2 == 0:
            a_width = a_lb_param.shape[1]
            b_width = b_w_glu.shape[1]
            z = _matmul(h, ab_w_in[j].astype(BF16), name="ab_in")
            o = _hgrn2(z, a_lb_param.astype(F32), j, batch, seq, a_width, a_width + b_width)
            tables = _s5_tables(b_a_re[j], b_a_im[j], b_log_dt[j], b_b_re[j], b_b_im[j],
                                b_c_re[j], b_c_im[j])
            yg = _s5(z, 4 * a_width, tables, b_d[j], batch, seq, b_width)
            o = _matmul(yg, b_w_glu[j].astype(BF16), extra=yg, epilogue="glu",
                        into=(o, a_width), name="s5_glu")
            xs = _matmul(o, ab_w_out[j].astype(BF16), extra=xs, epilogue="residual",
                         tm=512, name="ab_out")
        else:
            d_v = c_w_out.shape[1]
            d_qk = (c_w_in.shape[2] - 2 * d_v) // 2
            z = _matmul(h, c_w_in[j].astype(BF16), out_dtype=BF16, name="ret_in")
            o = _retention(z, batch, seq, d_qk, d_v)
            xs = _matmul(o, c_w_out[j].astype(BF16), extra=xs, epilogue="residual",
                         tm=512, tn=512, tk=8192, name="ret_out")
        hn = _rmsnorm(xs, ffn_norm_g[layer])
        xs = _peer(xs, hn, peer_w_q[layer], peer_sub_keys[layer], u16, v16, layer)
    out = _rmsnorm(xs, final_norm_g, out_dtype=F32)
    return out.reshape(batch, seq, d)
```
